```python
import jax, jax.numpy as jnp
from jax import lax
import numpy as np

D_MODEL = 4096
BATCH = 4
SEQ = 4096
DEPTH = 4

CHUNK = 64
N_LEFT_CHUNKS = 8
N_HEADS = 32
HEAD_DIM = D_MODEL // N_HEADS
MAX_REL = 128
CONV_WIDTH = 31
D_FF = 2 * D_MODEL
N_EXPERTS = 8
TOP_K = 2
D_FF_EXPERT = D_MODEL // 2
N_MIXERS = 2
N_EVEN_LAYERS = (DEPTH + 1) // 2
N_ODD_LAYERS = DEPTH // 2
RMS_EPS = 1e-6
LN_EPS = 1e-5

kernel_name = 'chunked_attn_conformer_conv_moe_hybrid'


def _rms_norm(x, g):
    xf = x.astype(jnp.float32)
    y = xf * lax.rsqrt(jnp.mean(xf * xf, axis=-1, keepdims=True) + RMS_EPS)
    return (y * g.astype(jnp.float32)).astype(x.dtype)


def _layer_norm(x, g, b):
    xf = x.astype(jnp.float32)
    mu = jnp.mean(xf, axis=-1, keepdims=True)
    var = jnp.mean(jnp.square(xf - mu), axis=-1, keepdims=True)
    y = (xf - mu) * lax.rsqrt(var + LN_EPS)
    return (y * g.astype(jnp.float32) + b.astype(jnp.float32)).astype(x.dtype)


def _chunked_attention(h, w_qkv, q_gain, k_gain, rel_bias, w_out):
    B, S, D = h.shape
    n_chunks = S // CHUNK
    left = N_LEFT_CHUNKS * CHUNK
    band = left + CHUNK
    q, k, v = jnp.split(h @ w_qkv, 3, axis=-1)
    q = _rms_norm(q.reshape(B, S, N_HEADS, HEAD_DIM), q_gain)
    k = _rms_norm(k.reshape(B, S, N_HEADS, HEAD_DIM), k_gain)
    v = v.reshape(B, S, N_HEADS, HEAD_DIM)
    pad = ((0, 0), (left, 0), (0, 0), (0, 0))
    k_pad = jnp.pad(k, pad)
    v_pad = jnp.pad(v, pad)
    q_loc = jnp.arange(CHUNK)[:, None]
    k_loc = jnp.arange(band)[None, :]
    rel_idx = jnp.clip(left + q_loc - k_loc, -MAX_REL, MAX_REL) + MAX_REL
    bias = rel_bias.astype(jnp.float32)[:, rel_idx]
    scale = HEAD_DIM ** -0.5

    def one_chunk(c):
        start = c * CHUNK
        q_c = lax.dynamic_slice_in_dim(q, start, CHUNK, axis=1)
        k_c = lax.dynamic_slice_in_dim(k_pad, start, band, axis=1)
        v_c = lax.dynamic_slice_in_dim(v_pad, start, band, axis=1)
        s = jnp.einsum('bqhd,bkhd->bhqk', q_c, k_c,
                       preferred_element_type=jnp.float32) * scale + bias
        valid = (start - left + jnp.arange(band)) >= 0
        s = jnp.where(valid, s, -jnp.inf)
        p = jax.nn.softmax(s, axis=-1).astype(v_c.dtype)
        return jnp.einsum('bhqk,bkhd->bqhd', p, v_c)

    o = lax.map(one_chunk, jnp.arange(n_chunks))
    o = jnp.moveaxis(o, 0, 1).reshape(B, S, D)
    return o @ w_out


def _conv_module(h, w_in, b_in, w_dw, b_dw, ln_g, ln_b, w_out, b_out):
    D = h.shape[-1]
    a, g = jnp.split(h @ w_in + b_in, 2, axis=-1)
    u = a * jax.nn.sigmoid(g)
    u = lax.conv_general_dilated(
        u, w_dw[:, None, :].astype(u.dtype), window_strides=(1,),
        padding=((CONV_WIDTH - 1, 0),),
        dimension_numbers=('NWC', 'WIO', 'NWC'),
        feature_group_count=D) + b_dw
    u = jax.nn.silu(_layer_norm(u, ln_g, ln_b))
    return u @ w_out + b_out


def _swiglu(h, w_gate_up, w_down):
    g, u = jnp.split(h @ w_gate_up, 2, axis=-1)
    return (jax.nn.silu(g) * u) @ w_down


def _moe_swiglu(h, w_router, w_gate, w_up, w_down):
    B, S, D = h.shape
    t = h.reshape(B * S, D)
    logits = jnp.dot(t, w_router, preferred_element_type=jnp.float32)
    top_val, top_idx = lax.top_k(logits, TOP_K)
    top_w = jax.nn.softmax(top_val, axis=-1)
    combine = jnp.einsum('tk,tke->te', top_w,
                         jax.nn.one_hot(top_idx, N_EXPERTS, dtype=jnp.float32))
    combine = combine.astype(t.dtype)
    y = jnp.zeros_like(t)
    for e in range(N_EXPERTS):
        he = jax.nn.silu(t @ w_gate[e]) * (t @ w_up[e])
        y = y + (combine[:, e:e + 1] * he) @ w_down[e]
    return y.reshape(B, S, D)


def _normal(key, shape, scale):
    return jax.random.normal(key, shape, jnp.float32) * scale


def setup_inputs(seed: int = 0) -> dict:
    key = jax.random.key(seed)
    ks = jax.random.split(key, 24)
    D = D_MODEL
    ne, no = N_EVEN_LAYERS, N_ODD_LAYERS
    return {
        'x': _normal(ks[0], (BATCH, SEQ, D), 1.0),
        'norm_mix': 1.0 + _normal(ks[1], (DEPTH, D), 0.02),
        'norm_ffn': 1.0 + _normal(ks[2], (DEPTH, D), 0.02),
        'attn_w_qkv': _normal(ks[3], (ne, D, 3 * D), D ** -0.5),
        'attn_q_norm': 1.0 + _normal(ks[4], (ne, HEAD_DIM), 0.02),
        'attn_k_norm': 1.0 + _normal(ks[5], (ne, HEAD_DIM), 0.02),
        'attn_rel_bias': _normal(ks[6], (ne, N_HEADS, 2 * MAX_REL + 1), 0.5),
        'attn_w_out': _normal(ks[7], (ne, D, D), D ** -0.5),
        'conv_w_in': _normal(ks[8], (no, D, 2 * D), D ** -0.5),
        'conv_b_in': _normal(ks[9], (no, 2 * D), 0.02),
        'conv_w_dw': _normal(ks[10], (no, CONV_WIDTH, D), CONV_WIDTH ** -0.5),
        'conv_b_dw': _normal(ks[11], (no, D), 0.02),
        'conv_ln_g': 1.0 + _normal(ks[12], (no, D), 0.02),
        'conv_ln_b': _normal(ks[13], (no, D), 0.02),
        'conv_w_out': _normal(ks[14], (no, D, D), D ** -0.5),
        'conv_b_out': _normal(ks[15], (no, D), 0.02),
        'ffn_w_gate_up': _normal(ks[16], (ne, D, 2 * D_FF), D ** -0.5),
        'ffn_w_down': _normal(ks[17], (ne, D_FF, D), D_FF ** -0.5),
        'moe_w_router': _normal(ks[18], (no, D, N_EXPERTS), D ** -0.5),
        'moe_w_gate': _normal(ks[19], (no, N_EXPERTS, D, D_FF_EXPERT), D ** -0.5),
        'moe_w_up': _normal(ks[20], (no, N_EXPERTS, D, D_FF_EXPERT), D ** -0.5),
        'moe_w_down': _normal(ks[21], (no, N_EXPERTS, D_FF_EXPERT, D), D_FF_EXPERT ** -0.5),
    }


def reference(x, norm_mix, norm_ffn, attn_w_qkv, attn_q_norm, attn_k_norm, attn_rel_bias,
              attn_w_out, conv_w_in, conv_b_in, conv_w_dw, conv_b_dw, conv_ln_g, conv_ln_b,
              conv_w_out, conv_b_out, ffn_w_gate_up, ffn_w_down, moe_w_router, moe_w_gate,
              moe_w_up, moe_w_down):
    h = x
    for i in range(DEPTH):
        j = i // N_MIXERS
        hn = _rms_norm(h, norm_mix[i])
        if i % N_MIXERS == 0:
            h = h + _chunked_attention(hn, attn_w_qkv[j], attn_q_norm[j], attn_k_norm[j],
                                       attn_rel_bias[j], attn_w_out[j])
        else:
            h = h + _conv_module(hn, conv_w_in[j], conv_b_in[j], conv_w_dw[j], conv_b_dw[j],
                                 conv_ln_g[j], conv_ln_b[j], conv_w_out[j], conv_b_out[j])
        hn = _rms_norm(h, norm_ffn[i])
        f = i // 2
        if i % 2 == 0:
            h = h + _swiglu(hn, ffn_w_gate_up[f], ffn_w_down[f])
        else:
            h = h + _moe_swiglu(hn, moe_w_router[f], moe_w_gate[f], moe_w_up[f], moe_w_down[f])
    return h
```

```python
import functools

import jax
import jax.numpy as jnp
from jax import lax
from jax.experimental import pallas as pl
from jax.experimental.pallas import tpu as pltpu

CHUNK = 64
N_LEFT_CHUNKS = 8
HEAD_DIM = 128
MAX_REL = 128
CONV_WIDTH = 31
TOP_K = 2
RMS_EPS = 1e-6
LN_EPS = 1e-5

LANES = 128
V7X_VMEM_LIMIT = 56 * 1024 * 1024
MASK_VALUE = -1e30

ATTN_Q_ROWS = 4 * CHUNK
ATTN_PIECES = 3
CONV_ROWS = 64
CONV_HALO = 32


def _params(sem, vmem=V7X_VMEM_LIMIT):
    return pltpu.CompilerParams(dimension_semantics=sem, vmem_limit_bytes=vmem)


def _rms_rows(x, g):
    ms = jnp.mean(x * x, axis=-1, keepdims=True)
    return x * lax.rsqrt(ms + RMS_EPS) * g


def _rmsnorm_kernel(x_ref, g_ref, o_ref):
    o_ref[...] = _rms_rows(x_ref[...], g_ref[...]).astype(o_ref.dtype)


def _rmsnorm(x, g, tm=256):
    m, d = x.shape
    tm = min(tm, m)
    return pl.pallas_call(
        _rmsnorm_kernel,
        grid=(m // tm,),
        in_specs=[pl.BlockSpec((tm, d), lambda i: (i, 0)),
                  pl.BlockSpec((1, d), lambda i: (0, 0))],
        out_specs=pl.BlockSpec((tm, d), lambda i: (i, 0)),
        out_shape=jax.ShapeDtypeStruct((m, d), jnp.bfloat16),
        compiler_params=_params(("parallel",)),
        name="rmsnorm",
    )(x, g.reshape(1, d))


def _mm_kernel(*refs, nk, has_bias, has_res):
    a_ref, b_ref = refs[0], refs[1]
    pos = 2
    bias_ref = res_ref = None
    if has_bias:
        bias_ref = refs[pos]
        pos += 1
    if has_res:
        res_ref = refs[pos]
        pos += 1
    o_ref = refs[pos]
    acc_ref = refs[pos + 1] if nk > 1 else None

    part = jnp.dot(a_ref[...], b_ref[...], preferred_element_type=jnp.float32)

    def finish(acc):
        if has_bias:
            acc = acc + bias_ref[...]
        if has_res:
            acc = acc + res_ref[...]
        o_ref[...] = acc.astype(o_ref.dtype)

    if nk == 1:
        finish(part)
    else:
        k = pl.program_id(2)

        @pl.when(k == 0)
        def _():
            acc_ref[...] = part

        @pl.when(jnp.logical_and(k > 0, k < nk - 1))
        def _():
            acc_ref[...] += part

        @pl.when(k == nk - 1)
        def _():
            finish(acc_ref[...] + part)


def _matmul(a, b, bias=None, res=None, out_dtype=jnp.float32, tm=1024, tn=512, tk=4096):
    m, kdim = a.shape
    n = b.shape[1]
    tm, tn, tk = min(tm, m), min(tn, n), min(tk, kdim)
    nk = kdim // tk
    in_specs = [pl.BlockSpec((tm, tk), lambda i, j, k: (i, k)),
                pl.BlockSpec((tk, tn), lambda i, j, k: (k, j))]
    args = [a, b]
    if bias is not None:
        in_specs.append(pl.BlockSpec((1, tn), lambda i, j, k: (0, j)))
        args.append(bias.reshape(1, n))
    if res is not None:
        in_specs.append(pl.BlockSpec((tm, tn), lambda i, j, k: (i, j)))
        args.append(res)
    scratch = [pltpu.VMEM((tm, tn), jnp.float32)] if nk > 1 else []
    return pl.pallas_call(
        functools.partial(_mm_kernel, nk=nk, has_bias=bias is not None, has_res=res is not None),
        grid=(m // tm, n // tn, nk),
        in_specs=in_specs,
        out_specs=pl.BlockSpec((tm, tn), lambda i, j, k: (i, j)),
        out_shape=jax.ShapeDtypeStruct((m, n), out_dtype),
        scratch_shapes=scratch,
        compiler_params=_params(("parallel", "parallel", "arbitrary")),
        name="matmul",
    )(*args)


def _glu_kernel(*refs, gate_first, has_bias):
    if has_bias:
        a_ref, b1_ref, b2_ref, c1_ref, c2_ref, o_ref = refs
    else:
        a_ref, b1_ref, b2_ref, o_ref = refs
    a = a_ref[...]
    y1 = jnp.dot(a, b1_ref[...], preferred_element_type=jnp.float32)
    y2 = jnp.dot(a, b2_ref[...], preferred_element_type=jnp.float32)
    if has_bias:
        y1 = y1 + c1_ref[...]
        y2 = y2 + c2_ref[...]
    if gate_first:
        out = y1 * jax.nn.sigmoid(y1) * y2
    else:
        out = y1 * jax.nn.sigmoid(y2)
    o_ref[...] = out.astype(o_ref.dtype)


def _glu_matmul(a, w, bias, gate_first, out_dtype, tm=1024, tn=512):
    m, kdim = a.shape
    n = w.shape[1] // 2
    tm, tn = min(tm, m), min(tn, n)
    nj = n // tn
    in_specs = [pl.BlockSpec((tm, kdim), lambda i, j: (i, 0)),
                pl.BlockSpec((kdim, tn), lambda i, j: (0, j)),
                pl.BlockSpec((kdim, tn), lambda i, j: (0, j + nj))]
    args = [a, w, w]
    if bias is not None:
        b2d = bias.reshape(1, 2 * n)
        in_specs += [pl.BlockSpec((1, tn), lambda i, j: (0, j)),
                     pl.BlockSpec((1, tn), lambda i, j: (0, j + nj))]
        args += [b2d, b2d]
    return pl.pallas_call(
        functools.partial(_glu_kernel, gate_first=gate_first, has_bias=bias is not None),
        grid=(m // tm, nj),
        in_specs=in_specs,
        out_specs=pl.BlockSpec((tm, tn), lambda i, j: (i, j)),
        out_shape=jax.ShapeDtypeStruct((m, n), out_dtype),
        compiler_params=_params(("parallel", "parallel")),
        name="glu_matmul",
    )(*args)


def _qkv_kernel(a_ref, b_ref, gain_ref, o_ref, *, n_norm_tiles, heads_per_tile):
    j = pl.program_id(1)
    acc = jnp.dot(a_ref[...], b_ref[...], preferred_element_type=jnp.float32)

    @pl.when(j < n_norm_tiles)
    def _():
        gain = gain_ref[0]
        for h in range(heads_per_tile):
            sl = slice(h * HEAD_DIM, (h + 1) * HEAD_DIM)
            o_ref[:, sl] = _rms_rows(acc[:, sl], gain).astype(o_ref.dtype)

    @pl.when(j >= n_norm_tiles)
    def _():
        o_ref[...] = acc.astype(o_ref.dtype)


def _qkv_proj(a, w, q_gain, k_gain, tm=1024, tn=1024):
    m, d = a.shape
    n = w.shape[1]
    tm, tn = min(tm, m), min(tn, d)
    tiles_per_section = d // tn
    gains = jnp.stack([q_gain * (HEAD_DIM ** -0.5), k_gain]).reshape(2, 1, HEAD_DIM)
    return pl.pallas_call(
        functools.partial(_qkv_kernel, n_norm_tiles=2 * tiles_per_section,
                          heads_per_tile=tn // HEAD_DIM),
        grid=(m // tm, n // tn),
        in_specs=[pl.BlockSpec((tm, d), lambda i, j: (i, 0)),
                  pl.BlockSpec((d, tn), lambda i, j: (0, j)),
                  pl.BlockSpec((1, 1, HEAD_DIM),
                               lambda i, j: (jnp.minimum(j // tiles_per_section, 1), 0, 0))],
        out_specs=pl.BlockSpec((tm, tn), lambda i, j: (i, j)),
        out_shape=jax.ShapeDtypeStruct((m, n), jnp.bfloat16),
        compiler_params=_params(("parallel", "parallel")),
        name="qkv_proj",
    )(a, w, gains)


def _attn_kernel(q_ref, k0_ref, k1_ref, k2_ref, v0_ref, v1_ref, v2_ref, bias_ref, o_ref,
                 *, heads):
    i = pl.program_id(2)
    rows = ATTN_Q_ROWS
    first_key = (i - (ATTN_PIECES - 1)) * rows
    col = lax.broadcasted_iota(jnp.int32, (1, rows), 1)
    k_refs = (k0_ref, k1_ref, k2_ref)
    v_refs = (v0_ref, v1_ref, v2_ref)
    valid = [first_key + p * rows + col >= 0 for p in range(ATTN_PIECES - 1)]
    for h in range(heads):
        sl = slice(h * HEAD_DIM, (h + 1) * HEAD_DIM)
        q = q_ref[:, sl]
        s = []
        for p in range(ATTN_PIECES):
            sp = lax.dot_general(q, k_refs[p][:, sl], (((1,), (1,)), ((), ())),
                                 preferred_element_type=jnp.float32)
            sp = sp + bias_ref[h, :, p * rows:(p + 1) * rows]
            if p < ATTN_PIECES - 1:
                sp = jnp.where(valid[p], sp, MASK_VALUE)
            s.append(sp)
        mx = jnp.max(s[0], axis=-1, keepdims=True)
        for p in range(1, ATTN_PIECES):
            mx = jnp.maximum(mx, jnp.max(s[p], axis=-1, keepdims=True))
        denom = None
        out = None
        for p in range(ATTN_PIECES):
            e = jnp.exp(s[p] - mx)
            dsum = jnp.sum(e, axis=-1, keepdims=True)
            pv = jnp.dot(e.astype(jnp.bfloat16), v_refs[p][:, sl],
                         preferred_element_type=jnp.float32)
            denom = dsum if denom is None else denom + dsum
            out = pv if out is None else out + pv
        o_ref[:, sl] = (out / denom).astype(o_ref.dtype)


def _band_bias(rel_bias):
    rows = ATTN_Q_ROWS
    qi = jnp.arange(rows)[:, None]
    ki = jnp.arange(ATTN_PIECES * rows)[None, :] - (ATTN_PIECES - 1) * rows
    rel = jnp.clip(qi - ki, -MAX_REL, MAX_REL) + MAX_REL
    qc = qi // CHUNK
    kc = jnp.floor_divide(ki, CHUNK)
    in_band = jnp.logical_and(kc <= qc, kc >= qc - N_LEFT_CHUNKS)
    bias = rel_bias.astype(jnp.float32)[:, rel]
    return jnp.where(in_band[None], bias, MASK_VALUE)


def _attention(qkv, rel_bias, batch, seq, heads_per_step=8):
    t, d3 = qkv.shape
    d = d3 // 3
    n_heads = d // HEAD_DIM
    hg = min(heads_per_step, n_heads)
    n_groups = n_heads // hg
    rows = ATTN_Q_ROWS
    nblk = seq // rows
    width = hg * HEAD_DIM
    bias = _band_bias(rel_bias)

    def q_map(g, b, i):
        return (b * nblk + i, g)

    def kv_map(section, back):
        def f(g, b, i):
            return (b * nblk + jnp.maximum(i - back, 0), section * n_groups + g)
        return f

    blk = lambda f: pl.BlockSpec((rows, width), f)
    return pl.pallas_call(
        functools.partial(_attn_kernel, heads=hg),
        grid=(n_groups, batch, nblk),
        in_specs=[blk(q_map),
                  blk(kv_map(1, 2)), blk(kv_map(1, 1)), blk(kv_map(1, 0)),
                  blk(kv_map(2, 2)), blk(kv_map(2, 1)), blk(kv_map(2, 0)),
                  pl.BlockSpec((hg, rows, ATTN_PIECES * rows), lambda g, b, i: (g, 0, 0))],
        out_specs=blk(q_map),
        out_shape=jax.ShapeDtypeStruct((t, d), jnp.bfloat16),
        compiler_params=_params(("parallel", "parallel", "parallel")),
        name="band_attention",
    )(qkv, qkv, qkv, qkv, qkv, qkv, qkv, bias)


def _conv_kernel(prev_ref, cur_ref, w_ref, bdw_ref, g_ref, b_ref, o_ref, buf_ref, cv_ref,
                 *, col_chunk, row_chunk):
    i = pl.program_id(1)
    d = cur_ref.shape[1]
    halo = CONV_HALO
    buf_ref[0:halo, :] = jnp.where(i > 0, prev_ref[...], 0.0)
    buf_ref[halo:halo + CONV_ROWS, :] = cur_ref[...]
    first = halo - (CONV_WIDTH - 1)
    for c0 in range(0, d, col_chunk):
        cs = slice(c0, c0 + col_chunk)
        for r0 in range(0, CONV_ROWS, row_chunk):
            acc = jnp.zeros((row_chunk, col_chunk), jnp.float32) + bdw_ref[:, cs]
            for t in range(CONV_WIDTH):
                start = first + r0 + t
                acc = acc + buf_ref[start:start + row_chunk, cs] * w_ref[t:t + 1, cs]
            cv_ref[r0:r0 + row_chunk, cs] = acc
    x = cv_ref[...]
    mu = jnp.mean(x, axis=-1, keepdims=True)
    xc = x - mu
    var = jnp.mean(xc * xc, axis=-1, keepdims=True)
    y = xc * lax.rsqrt(var + LN_EPS) * g_ref[...] + b_ref[...]
    o_ref[...] = (y * jax.nn.sigmoid(y)).astype(o_ref.dtype)


def _conv_ln_swish(u, w_dw, b_dw, ln_g, ln_b, batch, seq):
    t, d = u.shape
    nblk = seq // CONV_ROWS
    ratio = CONV_ROWS // CONV_HALO
    row = lambda v: v.reshape(1, d)
    return pl.pallas_call(
        functools.partial(_conv_kernel, col_chunk=min(512, d), row_chunk=32),
        grid=(batch, nblk),
        in_specs=[pl.BlockSpec((CONV_HALO, d),
                               lambda b, i: (jnp.maximum((b * nblk + i) * ratio - 1, 0), 0)),
                  pl.BlockSpec((CONV_ROWS, d), lambda b, i: (b * nblk + i, 0)),
                  pl.BlockSpec((CONV_WIDTH, d), lambda b, i: (0, 0)),
                  pl.BlockSpec((1, d), lambda b, i: (0, 0)),
                  pl.BlockSpec((1, d), lambda b, i: (0, 0)),
                  pl.BlockSpec((1, d), lambda b, i: (0, 0))],
        out_specs=pl.BlockSpec((CONV_ROWS, d), lambda b, i: (b * nblk + i, 0)),
        out_shape=jax.ShapeDtypeStruct((t, d), jnp.bfloat16),
        scratch_shapes=[pltpu.VMEM((CONV_HALO + CONV_ROWS, d), jnp.float32),
                        pltpu.VMEM((CONV_ROWS, d), jnp.float32)],
        compiler_params=_params(("parallel", "parallel")),
        name="conv_ln_swish",
    )(u, u, w_dw, row(b_dw), row(ln_g), row(ln_b))


def _router_kernel(x_ref, g_ref, w_ref, idx_ref, wt_ref, *, n_experts):
    hn = _rms_rows(x_ref[...], g_ref[...])
    logits = jnp.dot(hn, w_ref[...], preferred_element_type=jnp.float32,
                     precision=lax.Precision.HIGHEST)
    lane = lax.broadcasted_iota(jnp.int32, logits.shape, 1)
    logits = jnp.where(lane < n_experts, logits, -jnp.inf)
    v1 = jnp.max(logits, axis=-1, keepdims=True)
    i1 = jnp.min(jnp.where(logits == v1, lane, LANES), axis=-1, keepdims=True)
    rest = jnp.where(lane == i1, -jnp.inf, logits)
    v2 = jnp.max(rest, axis=-1, keepdims=True)
    i2 = jnp.min(jnp.where(rest == v2, lane, LANES), axis=-1, keepdims=True)
    e2 = jnp.exp(v2 - v1)
    denom = 1.0 + e2
    idx_ref[...] = jnp.where(lane == 0, i1, i2)
    wt_ref[...] = jnp.where(lane == 0, 1.0 / denom, e2 / denom)


def _router(h, g, w_router, tm=256):
    t, d = h.shape
    n_experts = w_router.shape[1]
    tm = min(tm, t)
    w_pad = jnp.zeros((d, LANES), jnp.float32).at[:, :n_experts].set(w_router)
    return pl.pallas_call(
        functools.partial(_router_kernel, n_experts=n_experts),
        grid=(t // tm,),
        in_specs=[pl.BlockSpec((tm, d), lambda i: (i, 0)),
                  pl.BlockSpec((1, d), lambda i: (0, 0)),
                  pl.BlockSpec((d, LANES), lambda i: (0, 0))],
        out_specs=[pl.BlockSpec((tm, LANES), lambda i: (i, 0)),
                   pl.BlockSpec((tm, LANES), lambda i: (i, 0))],
        out_shape=[jax.ShapeDtypeStruct((t, LANES), jnp.int32),
                   jax.ShapeDtypeStruct((t, LANES), jnp.float32)],
        compiler_params=_params(("parallel",)),
        name="router",
    )(h, g.reshape(1, d), w_pad)


def _row_copy(src_hbm, row, dst, slot, sem):
    return pltpu.make_async_copy(src_hbm.at[pl.ds(row, 1), :], dst.at[pl.ds(slot, 1), :], sem)


def _dispatch_kernel(src_ref, h_hbm, g_ref, o_ref, buf_ref, sem, *, rows):
    base = pl.program_id(0) * rows

    def issue(r, carry):
        _row_copy(h_hbm, src_ref[base + r], buf_ref, r, sem).start()
        return carry

    lax.fori_loop(0, rows, issue, 0)
    pltpu.make_async_copy(h_hbm.at[pl.ds(0, rows), :], buf_ref, sem).wait()
    o_ref[...] = _rms_rows(buf_ref[...], g_ref[...]).astype(o_ref.dtype)


def _dispatch(h, g, src, rows=256):
    t, d = h.shape
    p = src.shape[0]
    rows = min(rows, p)
    return pl.pallas_call(
        functools.partial(_dispatch_kernel, rows=rows),
        grid_spec=pltpu.PrefetchScalarGridSpec(
            num_scalar_prefetch=1,
            grid=(p // rows,),
            in_specs=[pl.BlockSpec(memory_space=pl.ANY),
                      pl.BlockSpec((1, d), lambda i, s: (0, 0))],
            out_specs=pl.BlockSpec((rows, d), lambda i, s: (i, 0)),
            scratch_shapes=[pltpu.VMEM((rows, d), jnp.float32),
                            pltpu.SemaphoreType.DMA(())]),
        out_shape=jax.ShapeDtypeStruct((p, d), jnp.bfloat16),
        compiler_params=_params(("arbitrary",)),
        name="moe_dispatch",
    )(src, h, g.reshape(1, d))


def _expert_up_kernel(te_ref, x_ref, wg_ref, wu_ref, o_ref):
    x = x_ref[...]
    g = jnp.dot(x, wg_ref[0], preferred_element_type=jnp.float32)
    u = jnp.dot(x, wu_ref[0], preferred_element_type=jnp.float32)
    o_ref[...] = (g * jax.nn.sigmoid(g) * u).astype(o_ref.dtype)


def _expert_up(xs, w_gate, w_up, tile_expert, tm, tf=512):
    p, d = xs.shape
    f = w_gate.shape[2]
    tf = min(tf, f)
    wspec = pl.BlockSpec((1, d, tf), lambda j, i, te: (te[i], 0, j))
    return pl.pallas_call(
        _expert_up_kernel,
        grid_spec=pltpu.PrefetchScalarGridSpec(
            num_scalar_prefetch=1,
            grid=(f // tf, p // tm),
            in_specs=[pl.BlockSpec((tm, d), lambda j, i, te: (i, 0)), wspec, wspec],
            out_specs=pl.BlockSpec((tm, tf), lambda j, i, te: (i, j))),
        out_shape=jax.ShapeDtypeStruct((p, f), jnp.bfloat16),
        compiler_params=_params(("parallel", "parallel")),
        name="expert_up",
    )(tile_expert, xs, w_gate, w_up)


def _expert_down_kernel(te_ref, a_ref, w_ref, o_ref):
    o_ref[...] = jnp.dot(a_ref[...], w_ref[0], preferred_element_type=jnp.float32)


def _expert_down(act, w_down, tile_expert, tm, tn=1024):
    p, f = act.shape
    d = w_down.shape[2]
    tn = min(tn, d)
    return pl.pallas_call(
        _expert_down_kernel,
        grid_spec=pltpu.PrefetchScalarGridSpec(
            num_scalar_prefetch=1,
            grid=(d // tn, p // tm),
            in_specs=[pl.BlockSpec((tm, f), lambda j, i, te: (i, 0)),
                      pl.BlockSpec((1, f, tn), lambda j, i, te: (te[i], 0, j))],
            out_specs=pl.BlockSpec((tm, tn), lambda j, i, te: (i, j))),
        out_shape=jax.ShapeDtypeStruct((p, d), jnp.float32),
        compiler_params=_params(("parallel", "parallel")),
        name="expert_down",
    )(tile_expert, act, w_down)


def _combine_kernel(pos_ref, ys_hbm, h_ref, wt_ref, o_ref, buf_ref, sem, *, rows):
    base = pl.program_id(0) * rows

    def issue(r, carry):
        for k in range(TOP_K):
            _row_copy(ys_hbm, pos_ref[TOP_K * (base + r) + k], buf_ref.at[k], r, sem).start()
        return carry

    lax.fori_loop(0, rows, issue, 0)
    for k in range(TOP_K):
        pltpu.make_async_copy(ys_hbm.at[pl.ds(0, rows), :], buf_ref.at[k], sem).wait()
    wt = wt_ref[...]
    acc = h_ref[...]
    for k in range(TOP_K):
        acc = acc + wt[:, k:k + 1] * buf_ref[k]
    o_ref[...] = acc


def _combine(h, ys, pos, wts, rows=128):
    t, d = h.shape
    rows = min(rows, t)
    return pl.pallas_call(
        functools.partial(_combine_kernel, rows=rows),
        grid_spec=pltpu.PrefetchScalarGridSpec(
            num_scalar_prefetch=1,
            grid=(t // rows,),
            in_specs=[pl.BlockSpec(memory_space=pl.ANY),
                      pl.BlockSpec((rows, d), lambda i, s: (i, 0)),
                      pl.BlockSpec((rows, LANES), lambda i, s: (i, 0))],
            out_specs=pl.BlockSpec((rows, d), lambda i, s: (i, 0)),
            scratch_shapes=[pltpu.VMEM((TOP_K, rows, d), jnp.float32),
                            pltpu.SemaphoreType.DMA(())]),
        out_shape=jax.ShapeDtypeStruct((t, d), jnp.float32),
        compiler_params=_params(("arbitrary",)),
        name="moe_combine",
    )(pos, ys, h, wts)


def _route_tables(idx, n_experts, tm):
    t = idx.shape[0]
    n_assign = t * TOP_K
    p = n_assign + n_experts * tm
    flat = idx.reshape(-1)
    onehot = (flat[:, None] == jnp.arange(n_experts)[None, :]).astype(jnp.int32)
    csum = jnp.cumsum(onehot, axis=0)
    counts = csum[-1]
    rank = jnp.sum(csum * onehot, axis=1) - 1
    padded = ((counts + tm - 1) // tm) * tm
    ends = jnp.cumsum(padded)
    starts = ends - padded
    pos = jnp.sum(starts[None, :] * onehot, axis=1) + rank
    src = jnp.zeros((p,), jnp.int32).at[pos].set(jnp.arange(n_assign, dtype=jnp.int32) // TOP_K)
    tile_start = jnp.arange(p // tm, dtype=jnp.int32) * tm
    tile_expert = jnp.sum((tile_start[:, None] >= ends[None, :]).astype(jnp.int32), axis=1)
    tile_expert = jnp.minimum(tile_expert, n_experts - 1)
    return pos.astype(jnp.int32), src, tile_expert.astype(jnp.int32)


def _moe(h, g, w_router, w_gate, w_up, w_down, tm=512):
    n_experts = w_router.shape[1]
    idx_pad, wts = _router(h, g, w_router)
    pos, src, tile_expert = _route_tables(idx_pad[:, :TOP_K], n_experts, tm)
    xs = _dispatch(h, g, src)
    act = _expert_up(xs, w_gate, w_up, tile_expert, tm)
    ys = _expert_down(act, w_down, tile_expert, tm)
    return _combine(h, ys, pos, wts)


def kernel(x, norm_mix, norm_ffn, attn_w_qkv, attn_q_norm, attn_k_norm, attn_rel_bias,
           attn_w_out, conv_w_in, conv_b_in, conv_w_dw, conv_b_dw, conv_ln_g, conv_ln_b,
           conv_w_out, conv_b_out, ffn_w_gate_up, ffn_w_down, moe_w_router, moe_w_gate,
           moe_w_up, moe_w_down):
    batch, seq, d = x.shape
    depth = norm_mix.shape[0]
    bf = lambda w: w.astype(jnp.bfloat16)
    h = x.reshape(batch * seq, d)
    for i in range(depth):
        j = i // 2
        hn = _rmsnorm(h, norm_mix[i])
        if i % 2 == 0:
            qkv = _qkv_proj(hn, bf(attn_w_qkv[j]), attn_q_norm[j], attn_k_norm[j])
            o = _attention(qkv, attn_rel_bias[j], batch, seq)
            h = _matmul(o, bf(attn_w_out[j]), res=h)
            hn = _rmsnorm(h, norm_ffn[i])
            act = _glu_matmul(hn, bf(ffn_w_gate_up[j]), None, True, jnp.bfloat16)
            h = _matmul(act, bf(ffn_w_down[j]), res=h)
        else:
            u = _glu_matmul(hn, bf(conv_w_in[j]), conv_b_in[j], False, jnp.float32)
            c = _conv_ln_swish(u, conv_w_dw[j], conv_b_dw[j], conv_ln_g[j], conv_ln_b[j],
                               batch, seq)
            h = _matmul(c, bf(conv_w_out[j]), bias=conv_b_out[j], res=h)
            h = _moe(h, norm_ffn[i], moe_w_router[j], bf(moe_w_gate[j]), bf(moe_w_up[j]),
                     bf(moe_w_down[j]))
    return h.reshape(batch, seq, d)
```

```python
import functools
import math

import jax
import jax.numpy as jnp
from jax import lax
from jax.experimental import pallas as pl
from jax.experimental.pallas import tpu as pltpu

CHUNK = 64
N_LEFT_CHUNKS = 8
HEAD_DIM = 128
MAX_REL = 128
CONV_WIDTH = 31
TOP_K = 2
RMS_EPS = 1e-6
LN_EPS = 1e-5

LANES = 128
SUBLANES = 8
V7X_VMEM_LIMIT = 56 * 1024 * 1024
MASK_VALUE = -1e30
LOG2E = math.log2(math.e)

ATTN_Q_ROWS = 4 * CHUNK
ATTN_PIECES = 3
ATTN_KEYS = ATTN_PIECES * ATTN_Q_ROWS
ATTN_TABLE = ATTN_KEYS + ATTN_Q_ROWS
CONV_ROWS = 64
CONV_HALO = 32


def _params(sem, vmem=V7X_VMEM_LIMIT):
    return pltpu.CompilerParams(dimension_semantics=sem, vmem_limit_bytes=vmem)


def _rms_rows(x, g):
    ms = jnp.mean(x * x, axis=-1, keepdims=True)
    return x * lax.rsqrt(ms + RMS_EPS) * g


def _rmsnorm_kernel(x_ref, g_ref, o_ref):
    o_ref[...] = _rms_rows(x_ref[...], g_ref[...]).astype(o_ref.dtype)


def _rmsnorm(x, g, tm=256):
    m, d = x.shape
    tm = min(tm, m)
    return pl.pallas_call(
        _rmsnorm_kernel,
        grid=(m // tm,),
        in_specs=[pl.BlockSpec((tm, d), lambda i: (i, 0)),
                  pl.BlockSpec((1, d), lambda i: (0, 0))],
        out_specs=pl.BlockSpec((tm, d), lambda i: (i, 0)),
        out_shape=jax.ShapeDtypeStruct((m, d), jnp.bfloat16),
        compiler_params=_params(("parallel",)),
        name="rmsnorm",
    )(x, g.reshape(1, d))


def _ws_kernel(*refs, n_w, mode, has_bias, has_res, grouped, heads_per_tile, n_norm_tiles):
    refs = list(refs)
    te_ref = nu_ref = res_ref = gain_ref = None
    if grouped:
        te_ref, nu_ref = refs.pop(0), refs.pop(0)
    a_ref = refs.pop(0)
    w_refs = [refs.pop(0) for _ in range(n_w)]
    b_refs = [refs.pop(0) for _ in range(n_w)] if has_bias else []
    if has_res:
        res_ref = refs.pop(0)
    if mode == "headnorm":
        gain_ref = refs.pop(0)
    o_ref = refs.pop(0)
    s_refs = refs
    j = pl.program_id(0)
    i = pl.program_id(1)

    if grouped:
        fresh = jnp.logical_or(i == 0, te_ref[i] != te_ref[jnp.maximum(i - 1, 0)])
    else:
        fresh = i == 0

    @pl.when(fresh)
    def _():
        for w_ref, s_ref in zip(w_refs, s_refs):
            s_ref[...] = w_ref[...].astype(jnp.bfloat16)

    def compute():
        a = a_ref[...]
        ys = [jnp.dot(a, s_ref[...], preferred_element_type=jnp.float32) for s_ref in s_refs]
        if has_bias:
            ys = [y + b_ref[...] for y, b_ref in zip(ys, b_refs)]
        if mode == "plain":
            y = ys[0]
            if has_res:
                y = y + res_ref[...]
            o_ref[...] = y.astype(o_ref.dtype)
        elif mode == "silu_mul":
            o_ref[...] = (ys[0] * jax.nn.sigmoid(ys[0]) * ys[1]).astype(o_ref.dtype)
        elif mode == "mul_sig":
            o_ref[...] = (ys[0] * jax.nn.sigmoid(ys[1])).astype(o_ref.dtype)
        else:
            y = ys[0]

            @pl.when(j < n_norm_tiles)
            def _():
                gain = gain_ref[...]
                for h in range(heads_per_tile):
                    sl = slice(h * HEAD_DIM, (h + 1) * HEAD_DIM)
                    o_ref[:, sl] = _rms_rows(y[:, sl], gain).astype(o_ref.dtype)

            @pl.when(j >= n_norm_tiles)
            def _():
                o_ref[...] = y.astype(o_ref.dtype)

    if grouped:
        @pl.when(i < nu_ref[0])
        def _():
            compute()

        @pl.when(i >= nu_ref[0])
        def _():
            o_ref[...] = jnp.zeros(o_ref.shape, o_ref.dtype)
    else:
        compute()


def _ws_matmul(a, weights, layer, *, n_out, tm, tn, mode="plain", biases=None, res=None,
               gains=None, out_dtype=jnp.float32, tables=None, single_buffer_w=False):
    m, kdim = a.shape
    tm, tn = min(tm, m), min(tn, n_out)
    nj, ni = n_out // tn, m // tm
    grouped = tables is not None
    n_w = len(weights)
    wmode = dict(pipeline_mode=pl.Buffered(1)) if single_buffer_w else {}

    def w_spec(col0):
        off = col0 // tn
        if grouped:
            return pl.BlockSpec((None, None, kdim, tn),
                                lambda j, i, te, nu: (layer, te[i], 0, j + off), **wmode)
        return pl.BlockSpec((None, kdim, tn), lambda j, i: (layer, 0, j + off), **wmode)

    def spec(shape, f):
        if grouped:
            return pl.BlockSpec(shape, lambda j, i, te, nu: f(j, i))
        return pl.BlockSpec(shape, f)

    in_specs = [spec((tm, kdim), lambda j, i: (i, 0))]
    args = [a]
    for w, col0 in weights:
        in_specs.append(w_spec(col0))
        args.append(w)
    if biases is not None:
        for b, col0 in biases:
            off = col0 // tn
            in_specs.append(spec((None, 1, tn), lambda j, i, off=off: (layer, 0, j + off)))
            args.append(b)
    if res is not None:
        in_specs.append(spec((tm, tn), lambda j, i: (i, j)))
        args.append(res)
    heads_per_tile = n_norm_tiles = 0
    if mode == "headnorm":
        heads_per_tile = tn // HEAD_DIM
        tiles_per_section = (n_out // 3) // tn
        n_norm_tiles = 2 * tiles_per_section
        in_specs.append(spec((None, 1, HEAD_DIM),
                             lambda j, i: (jnp.minimum(j // tiles_per_section, 1), 0, 0)))
        args.append(gains)
    body = functools.partial(_ws_kernel, n_w=n_w, mode=mode, has_bias=biases is not None,
                             has_res=res is not None, grouped=grouped,
                             heads_per_tile=heads_per_tile, n_norm_tiles=n_norm_tiles)
    scratch = [pltpu.VMEM((kdim, tn), jnp.bfloat16) for _ in range(n_w)]
    out_shape = jax.ShapeDtypeStruct((m, n_out), out_dtype)
    cparams = _params(("arbitrary", "arbitrary"))
    if grouped:
        return pl.pallas_call(
            body,
            grid_spec=pltpu.PrefetchScalarGridSpec(
                num_scalar_prefetch=2, grid=(nj, ni), in_specs=in_specs,
                out_specs=spec((tm, tn), lambda j, i: (i, j)), scratch_shapes=scratch),
            out_shape=out_shape, compiler_params=cparams, name="ws_matmul_" + mode + "_grouped",
        )(*tables, *args)
    return pl.pallas_call(
        body, grid=(nj, ni), in_specs=in_specs,
        out_specs=pl.BlockSpec((tm, tn), lambda j, i: (i, j)),
        out_shape=out_shape, scratch_shapes=scratch, compiler_params=cparams,
        name="ws_matmul_" + mode,
    )(*args)


def _attn_kernel(q_ref, k0_ref, k1_ref, k2_ref, v0_ref, v1_ref, v2_ref, tab_ref, o_ref,
                 bias_ref, *, heads):
    b = pl.program_id(1)
    i = pl.program_id(2)
    rows = ATTN_Q_ROWS

    @pl.when(jnp.logical_and(b == 0, i == 0))
    def _():
        qi = lax.broadcasted_iota(jnp.int32, (rows, ATTN_KEYS), 0) // CHUNK
        kc = lax.broadcasted_iota(jnp.int32, (rows, ATTN_KEYS), 1) // CHUNK
        in_band = jnp.logical_and(kc >= qi, kc <= qi + N_LEFT_CHUNKS)
        for h in range(heads):
            tab = jnp.broadcast_to(tab_ref[h], (rows, ATTN_TABLE))
            rolled = pltpu.roll(tab, 0, 1, stride=1, stride_axis=0)
            bias_ref[h] = jnp.where(in_band, rolled[:, :ATTN_KEYS], MASK_VALUE)

    k_refs = (k0_ref, k1_ref, k2_ref)
    v_refs = (v0_ref, v1_ref, v2_ref)

    def scores(h):
        sl = slice(h * HEAD_DIM, (h + 1) * HEAD_DIM)
        q = q_ref[:, sl]
        s = []
        for p in range(ATTN_PIECES):
            sp = lax.dot_general(q, k_refs[p][:, sl], (((1,), (1,)), ((), ())),
                                 preferred_element_type=jnp.float32)
            sp = sp + bias_ref[h, :, p * rows:(p + 1) * rows]
            if p < ATTN_PIECES - 1:
                sp = jnp.where(i + p >= ATTN_PIECES - 1, sp, MASK_VALUE)
            s.append(sp)
        return s

    s_next = scores(0)
    for h in range(heads):
        sl = slice(h * HEAD_DIM, (h + 1) * HEAD_DIM)
        s = s_next
        if h + 1 < heads:
            s_next = scores(h + 1)
        mx = jnp.max(s[0], axis=-1, keepdims=True)
        for p in range(1, ATTN_PIECES):
            mx = jnp.maximum(mx, jnp.max(s[p], axis=-1, keepdims=True))
        denom = None
        out = None
        for p in range(ATTN_PIECES):
            e = jnp.exp2(s[p] - mx)
            dsum = jnp.sum(e, axis=-1, keepdims=True)
            pv = jnp.dot(e.astype(jnp.bfloat16), v_refs[p][:, sl],
                         preferred_element_type=jnp.float32)
            denom = dsum if denom is None else denom + dsum
            out = pv if out is None else out + pv
        o_ref[:, sl] = (out / denom).astype(o_ref.dtype)


def _bias_rows(rel_bias):
    c = jnp.arange(ATTN_TABLE)
    dist = (ATTN_PIECES - 1) * ATTN_Q_ROWS - c
    dist = jnp.where(c < ATTN_KEYS, dist, MAX_REL)
    rel = jnp.clip(dist, -MAX_REL, MAX_REL) + MAX_REL
    rows = rel_bias.astype(jnp.float32)[:, rel] * LOG2E
    return rows.reshape(rel_bias.shape[0], 1, ATTN_TABLE)


def _attention(qkv, rel_bias, batch, seq, heads_per_step=8):
    t, d3 = qkv.shape
    d = d3 // 3
    n_heads = d // HEAD_DIM
    hg = min(heads_per_step, n_heads)
    n_groups = n_heads // hg
    rows = ATTN_Q_ROWS
    nblk = seq // rows
    width = hg * HEAD_DIM

    def q_map(g, b, i):
        return (b * nblk + i, g)

    def kv_map(section, back):
        def f(g, b, i):
            return (b * nblk + jnp.maximum(i - back, 0), section * n_groups + g)
        return f

    blk = lambda f: pl.BlockSpec((rows, width), f)
    return pl.pallas_call(
        functools.partial(_attn_kernel, heads=hg),
        grid=(n_groups, batch, nblk),
        in_specs=[blk(q_map),
                  blk(kv_map(1, 2)), blk(kv_map(1, 1)), blk(kv_map(1, 0)),
                  blk(kv_map(2, 2)), blk(kv_map(2, 1)), blk(kv_map(2, 0)),
                  pl.BlockSpec((hg, 1, ATTN_TABLE), lambda g, b, i: (g, 0, 0))],
        out_specs=blk(q_map),
        out_shape=jax.ShapeDtypeStruct((t, d), jnp.bfloat16),
        scratch_shapes=[pltpu.VMEM((hg, rows, ATTN_KEYS), jnp.float32)],
        compiler_params=_params(("arbitrary", "arbitrary", "arbitrary")),
        name="band_attention",
    )(qkv, qkv, qkv, qkv, qkv, qkv, qkv, _bias_rows(rel_bias))


def _conv_kernel(prev_ref, cur_ref, w_ref, bdw_ref, g_ref, b_ref, o_ref, buf_ref, cv_ref,
                 *, col_chunk):
    i = pl.program_id(1)
    d = cur_ref.shape[1]
    halo = CONV_HALO
    buf_ref[0:halo, :] = jnp.where(i > 0, prev_ref[...], 0.0)
    buf_ref[halo:halo + CONV_ROWS, :] = cur_ref[...]
    first = halo - (CONV_WIDTH - 1)
    for c0 in range(0, d, col_chunk):
        cs = slice(c0, c0 + col_chunk)
        acc = jnp.zeros((CONV_ROWS, col_chunk), jnp.float32) + bdw_ref[:, cs]
        for s in range(SUBLANES):
            taps = [t for t in range(CONV_WIDTH) if (first + t) % SUBLANES == s]
            n_rows = CONV_ROWS + (SUBLANES if s else 0)
            part = None
            for t in taps:
                start = first + t - s
                wt = w_ref[t * SUBLANES:(t + 1) * SUBLANES, cs][None]
                term = buf_ref[start:start + n_rows, cs].reshape(-1, SUBLANES, col_chunk) * wt
                part = term if part is None else part + term
            part = part.reshape(n_rows, col_chunk)
            if s:
                part = pltpu.roll(part, n_rows - s, 0)[:CONV_ROWS, :]
            acc = acc + part
        cv_ref[:, cs] = acc
    x = cv_ref[...]
    mu = jnp.mean(x, axis=-1, keepdims=True)
    xc = x - mu
    var = jnp.mean(xc * xc, axis=-1, keepdims=True)
    y = xc * lax.rsqrt(var + LN_EPS) * g_ref[...] + b_ref[...]
    o_ref[...] = (y * jax.nn.sigmoid(y)).astype(o_ref.dtype)


def _conv_ln_swish(u, w_dw, b_dw, ln_g, ln_b, layer, batch, seq):
    t, d = u.shape
    nblk = seq // CONV_ROWS
    ratio = CONV_ROWS // CONV_HALO
    row3 = lambda v: v.reshape(v.shape[0], 1, d)
    vec = pl.BlockSpec((None, 1, d), lambda b, i: (layer, 0, 0))
    return pl.pallas_call(
        functools.partial(_conv_kernel, col_chunk=min(256, d)),
        grid=(batch, nblk),
        in_specs=[pl.BlockSpec((CONV_HALO, d),
                               lambda b, i: (jnp.maximum((b * nblk + i) * ratio - 1, 0), 0)),
                  pl.BlockSpec((CONV_ROWS, d), lambda b, i: (b * nblk + i, 0)),
                  pl.BlockSpec((None, CONV_WIDTH * SUBLANES, d), lambda b, i: (layer, 0, 0)),
                  vec, vec, vec],
        out_specs=pl.BlockSpec((CONV_ROWS, d), lambda b, i: (b * nblk + i, 0)),
        out_shape=jax.ShapeDtypeStruct((t, d), jnp.bfloat16),
        scratch_shapes=[pltpu.VMEM((CONV_HALO + CONV_ROWS, d), jnp.float32),
                        pltpu.VMEM((CONV_ROWS, d), jnp.float32)],
        compiler_params=_params(("parallel", "parallel")),
        name="conv_ln_swish",
    )(u, u, jnp.repeat(w_dw, SUBLANES, axis=1), row3(b_dw), row3(ln_g), row3(ln_b))


def _router_kernel(x_ref, g_ref, w_ref, idx_ref, wt_ref, *, n_experts):
    hn = _rms_rows(x_ref[...], g_ref[...])
    logits = jnp.dot(hn, w_ref[...], preferred_element_type=jnp.float32,
                     precision=lax.Precision.HIGHEST)
    lane = lax.broadcasted_iota(jnp.int32, logits.shape, 1)
    logits = jnp.where(lane < n_experts, logits, -jnp.inf)
    v1 = jnp.max(logits, axis=-1, keepdims=True)
    i1 = jnp.min(jnp.where(logits == v1, lane, LANES), axis=-1, keepdims=True)
    rest = jnp.where(lane == i1, -jnp.inf, logits)
    v2 = jnp.max(rest, axis=-1, keepdims=True)
    i2 = jnp.min(jnp.where(rest == v2, lane, LANES), axis=-1, keepdims=True)
    e2 = jnp.exp(v2 - v1)
    denom = 1.0 + e2
    idx_ref[...] = jnp.where(lane == 0, i1, i2)
    wt_ref[...] = jnp.where(lane == 0, 1.0 / denom, e2 / denom)


def _router(h, g, w_router, tm=256):
    t, d = h.shape
    n_experts = w_router.shape[1]
    tm = min(tm, t)
    w_pad = jnp.zeros((d, LANES), jnp.float32).at[:, :n_experts].set(w_router)
    return pl.pallas_call(
        functools.partial(_router_kernel, n_experts=n_experts),
        grid=(t // tm,),
        in_specs=[pl.BlockSpec((tm, d), lambda i: (i, 0)),
                  pl.BlockSpec((1, d), lambda i: (0, 0)),
                  pl.BlockSpec((d, LANES), lambda i: (0, 0))],
        out_specs=[pl.BlockSpec((tm, LANES), lambda i: (i, 0)),
                   pl.BlockSpec((tm, LANES), lambda i: (i, 0))],
        out_shape=[jax.ShapeDtypeStruct((t, LANES), jnp.int32),
                   jax.ShapeDtypeStruct((t, LANES), jnp.float32)],
        compiler_params=_params(("parallel",)),
        name="router",
    )(h, g.reshape(1, d), w_pad)


def _row_copy(src_hbm, row, dst, slot, sem):
    return pltpu.make_async_copy(src_hbm.at[pl.ds(row, 1), :], dst.at[pl.ds(slot, 1), :], sem)


def _dispatch_kernel(src_ref, nu_ref, h_hbm, g_ref, o_ref, buf_ref, sem, *, rows, tiles_per_mm):
    step = pl.program_id(0)
    base = step * rows
    used = step < nu_ref[0] * tiles_per_mm

    @pl.when(used)
    def _():
        def issue(r, carry):
            _row_copy(h_hbm, src_ref[base + r], buf_ref, r, sem).start()
            return carry

        lax.fori_loop(0, rows, issue, 0)
        pltpu.make_async_copy(h_hbm.at[pl.ds(0, rows), :], buf_ref, sem).wait()
        o_ref[...] = _rms_rows(buf_ref[...], g_ref[...]).astype(o_ref.dtype)

    @pl.when(jnp.logical_not(used))
    def _():
        o_ref[...] = jnp.zeros(o_ref.shape, o_ref.dtype)


def _dispatch(h, g, src, n_used, tm, rows=256):
    t, d = h.shape
    p = src.shape[0]
    rows = min(rows, tm)
    return pl.pallas_call(
        functools.partial(_dispatch_kernel, rows=rows, tiles_per_mm=tm // rows),
        grid_spec=pltpu.PrefetchScalarGridSpec(
            num_scalar_prefetch=2,
            grid=(p // rows,),
            in_specs=[pl.BlockSpec(memory_space=pl.ANY),
                      pl.BlockSpec((1, d), lambda i, s, nu: (0, 0))],
            out_specs=pl.BlockSpec((rows, d), lambda i, s, nu: (i, 0)),
            scratch_shapes=[pltpu.VMEM((rows, d), jnp.float32),
                            pltpu.SemaphoreType.DMA(())]),
        out_shape=jax.ShapeDtypeStruct((p, d), jnp.bfloat16),
        compiler_params=_params(("arbitrary",)),
        name="moe_dispatch",
    )(src, n_used, h, g.reshape(1, d))


def _combine_kernel(pos_ref, ys_hbm, h_ref, wt_ref, o_ref, buf_ref, sem, *, rows):
    base = pl.program_id(0) * rows

    def issue(r, carry):
        for k in range(TOP_K):
            _row_copy(ys_hbm, pos_ref[TOP_K * (base + r) + k], buf_ref.at[k], r, sem).start()
        return carry

    lax.fori_loop(0, rows, issue, 0)
    for k in range(TOP_K):
        pltpu.make_async_copy(ys_hbm.at[pl.ds(0, rows), :], buf_ref.at[k], sem).wait()
    wt = wt_ref[...]
    acc = h_ref[...]
    for k in range(TOP_K):
        acc = acc + wt[:, k:k + 1] * buf_ref[k]
    o_ref[...] = acc


def _combine(h, ys, pos, wts, rows=128):
    t, d = h.shape
    rows = min(rows, t)
    return pl.pallas_call(
        functools.partial(_combine_kernel, rows=rows),
        grid_spec=pltpu.PrefetchScalarGridSpec(
            num_scalar_prefetch=1,
            grid=(t // rows,),
            in_specs=[pl.BlockSpec(memory_space=pl.ANY),
                      pl.BlockSpec((rows, d), lambda i, s: (i, 0)),
                      pl.BlockSpec((rows, LANES), lambda i, s: (i, 0))],
            out_specs=pl.BlockSpec((rows, d), lambda i, s: (i, 0)),
            scratch_shapes=[pltpu.VMEM((TOP_K, rows, d), jnp.float32),
                            pltpu.SemaphoreType.DMA(())]),
        out_shape=jax.ShapeDtypeStruct((t, d), jnp.float32),
        compiler_params=_params(("arbitrary",)),
        name="moe_combine",
    )(pos, ys, h, wts)


def _route_tables(idx, n_experts, tm):
    t = idx.shape[0]
    n_assign = t * TOP_K
    p = n_assign + n_experts * tm
    flat = idx.reshape(-1)
    onehot = (flat[:, None] == jnp.arange(n_experts)[None, :]).astype(jnp.int32)
    csum = jnp.cumsum(onehot, axis=0)
    counts = csum[-1]
    rank = jnp.sum(csum * onehot, axis=1) - 1
    padded = ((counts + tm - 1) // tm) * tm
    ends = jnp.cumsum(padded)
    starts = ends - padded
    pos = jnp.sum(starts[None, :] * onehot, axis=1) + rank
    src = jnp.zeros((p,), jnp.int32).at[pos].set(jnp.arange(n_assign, dtype=jnp.int32) // TOP_K)
    tile_start = jnp.arange(p // tm, dtype=jnp.int32) * tm
    tile_expert = jnp.sum((tile_start[:, None] >= ends[None, :]).astype(jnp.int32), axis=1)
    tile_expert = jnp.minimum(tile_expert, n_experts - 1)
    n_used = (ends[-1:] // tm).astype(jnp.int32)
    return pos.astype(jnp.int32), src, tile_expert.astype(jnp.int32), n_used


def _moe(h, g, w_router, w_gate, w_up, w_down, layer, tm=512):
    n_experts = w_router.shape[1]
    f = w_gate.shape[3]
    d = h.shape[1]
    idx_pad, wts = _router(h, g, w_router)
    pos, src, tile_expert, n_used = _route_tables(idx_pad[:, :TOP_K], n_experts, tm)
    xs = _dispatch(h, g, src, n_used, tm)
    tables = (tile_expert, n_used)
    act = _ws_matmul(xs, [(w_gate, 0), (w_up, 0)], layer, n_out=f, tm=tm, tn=512,
                     mode="silu_mul", out_dtype=jnp.bfloat16, tables=tables)
    ys = _ws_matmul(act, [(w_down, 0)], layer, n_out=d, tm=tm, tn=1024, tables=tables)
    return _combine(h, ys, pos, wts)


def kernel(x, norm_mix, norm_ffn, attn_w_qkv, attn_q_norm, attn_k_norm, attn_rel_bias,
           attn_w_out, conv_w_in, conv_b_in, conv_w_dw, conv_b_dw, conv_ln_g, conv_ln_b,
           conv_w_out, conv_b_out, ffn_w_gate_up, ffn_w_down, moe_w_router, moe_w_gate,
           moe_w_up, moe_w_down):
    batch, seq, d = x.shape
    depth = norm_mix.shape[0]
    d_ff = ffn_w_gate_up.shape[2] // 2
    bf16 = jnp.bfloat16
    row3 = lambda v: v.reshape(v.shape[0], 1, v.shape[1])
    conv_b_in3, conv_b_out3 = row3(conv_b_in), row3(conv_b_out)
    h = x.reshape(batch * seq, d)
    for i in range(depth):
        j = i // 2
        hn = _rmsnorm(h, norm_mix[i])
        if i % 2 == 0:
            gains = jnp.stack([attn_q_norm[j] * (HEAD_DIM ** -0.5 * LOG2E),
                               attn_k_norm[j]]).reshape(2, 1, HEAD_DIM)
            qkv = _ws_matmul(hn, [(attn_w_qkv, 0)], j, n_out=3 * d, tm=1024, tn=512,
                             mode="headnorm", gains=gains, out_dtype=bf16)
            o = _attention(qkv, attn_rel_bias[j], batch, seq)
            h = _ws_matmul(o, [(attn_w_out, 0)], j, n_out=d, tm=1024, tn=512, res=h)
            hn = _rmsnorm(h, norm_ffn[i])
            act = _ws_matmul(hn, [(ffn_w_gate_up, 0), (ffn_w_gate_up, d_ff)], j, n_out=d_ff,
                             tm=1024, tn=256, mode="silu_mul", out_dtype=bf16)
            h = _ws_matmul(act, [(ffn_w_down, 0)], j, n_out=d, tm=512, tn=512, res=h,
                           single_buffer_w=True)
        else:
            u = _ws_matmul(hn, [(conv_w_in, 0), (conv_w_in, d)], j, n_out=d, tm=1024, tn=256,
                           mode="mul_sig", biases=[(conv_b_in3, 0), (conv_b_in3, d)])
            c = _conv_ln_swish(u, conv_w_dw, conv_b_dw, conv_ln_g, conv_ln_b, j, batch, seq)
            h = _ws_matmul(c, [(conv_w_out, 0)], j, n_out=d, tm=1024, tn=512,
                           biases=[(conv_b_out3, 0)], res=h)
            h = _moe(h, norm_ffn[i], moe_w_router[j], moe_w_gate, moe_w_up, moe_w_down, j)
    return h.reshape(batch, seq, d)
```

```python
import functools
import math

import jax
import jax.numpy as jnp
from jax import lax
from jax.experimental import pallas as pl
from jax.experimental.pallas import tpu as pltpu

CHUNK = 64
N_LEFT_CHUNKS = 8
HEAD_DIM = 128
MAX_REL = 128
CONV_WIDTH = 31
TOP_K = 2
RMS_EPS = 1e-6
LN_EPS = 1e-5

LANES = 128
SUBLANES = 8
V7X_VMEM_LIMIT = 56 * 1024 * 1024
MASK_VALUE = -1e30
LOG2E = math.log2(math.e)

ATTN_Q_ROWS = 4 * CHUNK
ATTN_PIECES = 3
ATTN_KEYS = ATTN_PIECES * ATTN_Q_ROWS
ATTN_TABLE = ATTN_KEYS + ATTN_Q_ROWS
CONV_ROWS = 64
CONV_HALO = 32


def _params(sem, vmem=V7X_VMEM_LIMIT):
    return pltpu.CompilerParams(dimension_semantics=sem, vmem_limit_bytes=vmem)


def _rms_rows(x, g):
    ms = jnp.mean(x * x, axis=-1, keepdims=True)
    return x * lax.rsqrt(ms + RMS_EPS) * g


def _rmsnorm_kernel(x_ref, g_ref, o_ref):
    o_ref[...] = _rms_rows(x_ref[...], g_ref[...]).astype(o_ref.dtype)


def _rmsnorm(x, g, tm=256):
    m, d = x.shape
    tm = min(tm, m)
    return pl.pallas_call(
        _rmsnorm_kernel,
        grid=(m // tm,),
        in_specs=[pl.BlockSpec((tm, d), lambda i: (i, 0)),
                  pl.BlockSpec((1, d), lambda i: (0, 0))],
        out_specs=pl.BlockSpec((tm, d), lambda i: (i, 0)),
        out_shape=jax.ShapeDtypeStruct((m, d), jnp.bfloat16),
        compiler_params=_params(("parallel",)),
        name="rmsnorm",
    )(x, g.reshape(1, d))


def _epilogue(ys, j, o_ref, b_refs, res_ref, gain_ref, *, mode, heads_per_tile, n_norm_tiles):
    if b_refs:
        ys = [y + b_ref[...] for y, b_ref in zip(ys, b_refs)]
    if mode == "plain":
        y = ys[0]
        if res_ref is not None:
            y = y + res_ref[...]
        o_ref[...] = y.astype(o_ref.dtype)
    elif mode == "silu_mul":
        o_ref[...] = (ys[0] * jax.nn.sigmoid(ys[0]) * ys[1]).astype(o_ref.dtype)
    elif mode == "mul_sig":
        o_ref[...] = (ys[0] * jax.nn.sigmoid(ys[1])).astype(o_ref.dtype)
    else:
        y = ys[0]

        @pl.when(j < n_norm_tiles)
        def _():
            gain = gain_ref[...]
            for h in range(heads_per_tile):
                sl = slice(h * HEAD_DIM, (h + 1) * HEAD_DIM)
                o_ref[:, sl] = _rms_rows(y[:, sl], gain).astype(o_ref.dtype)

        @pl.when(j >= n_norm_tiles)
        def _():
            o_ref[...] = y.astype(o_ref.dtype)


def _split_refs(refs, n_w, has_bias, has_res, mode):
    refs = list(refs)
    a_ref = refs.pop(0)
    w_refs = [refs.pop(0) for _ in range(n_w)]
    b_refs = [refs.pop(0) for _ in range(n_w)] if has_bias else []
    res_ref = refs.pop(0) if has_res else None
    gain_ref = refs.pop(0) if mode == "headnorm" else None
    o_ref = refs.pop(0)
    return a_ref, w_refs, b_refs, res_ref, gain_ref, o_ref, refs


def _dense_kernel(*refs, n_w, mode, has_bias, has_res, nj, kc, **epi):
    a_ref, w_refs, b_refs, res_ref, gain_ref, o_ref, s_refs = _split_refs(
        refs, n_w, has_bias, has_res, mode)
    jj = pl.program_id(0)
    i = pl.program_id(1)
    rows = pl.ds(pl.multiple_of(i * kc, kc), kc)

    def convert(parity):
        for t, w_ref in enumerate(w_refs):
            s_refs[2 * t + parity][rows, :] = w_ref[...].astype(jnp.bfloat16)

    def multiply(parity):
        a = a_ref[...]
        ys = [jnp.dot(a, s_refs[2 * t + parity][...], preferred_element_type=jnp.float32)
              for t in range(n_w)]
        _epilogue(ys, jj - 1, o_ref, b_refs, res_ref, gain_ref, mode=mode, **epi)

    @pl.when(jj == 0)
    def _():
        convert(0)

    for parity in range(2):
        @pl.when(jnp.logical_and(jj > 0, jj % 2 == parity))
        def _():
            convert(parity)
            multiply(1 - parity)


def _dense_matmul(a, weights, layer, *, n_out, tm, tn, mode="plain", biases=None, res=None,
                  gains=None, out_dtype=jnp.float32):
    m, kdim = a.shape
    tm, tn = min(tm, m), min(tn, n_out)
    nj, ni = n_out // tn, m // tm
    kc = kdim // ni
    assert kc * ni == kdim and kc % 16 == 0, (kdim, ni)
    n_w = len(weights)
    col = lambda jj: jnp.maximum(jj - 1, 0)
    row = lambda jj, i: jnp.where(jj == 0, 0, i)

    in_specs = [pl.BlockSpec((tm, kdim), lambda jj, i: (row(jj, i), 0))]
    args = [a]
    for w, col0 in weights:
        off = col0 // tn
        in_specs.append(pl.BlockSpec(
            (None, kc, tn), lambda jj, i, off=off: (layer, i, jnp.minimum(jj, nj - 1) + off)))
        args.append(w)
    if biases is not None:
        for b, col0 in biases:
            off = col0 // tn
            in_specs.append(pl.BlockSpec((None, 1, tn),
                                         lambda jj, i, off=off: (layer, 0, col(jj) + off)))
            args.append(b)
    if res is not None:
        in_specs.append(pl.BlockSpec((tm, tn), lambda jj, i: (row(jj, i), col(jj))))
        args.append(res)
    heads_per_tile = n_norm_tiles = 0
    if mode == "headnorm":
        heads_per_tile = tn // HEAD_DIM
        tiles_per_section = (n_out // 3) // tn
        n_norm_tiles = 2 * tiles_per_section
        in_specs.append(pl.BlockSpec(
            (None, 1, HEAD_DIM),
            lambda jj, i: (jnp.minimum(col(jj) // tiles_per_section, 1), 0, 0)))
        args.append(gains)
    body = functools.partial(_dense_kernel, n_w=n_w, mode=mode, has_bias=biases is not None,
                             has_res=res is not None, nj=nj, kc=kc,
                             heads_per_tile=heads_per_tile, n_norm_tiles=n_norm_tiles)
    return pl.pallas_call(
        body, grid=(nj + 1, ni), in_specs=in_specs,
        out_specs=pl.BlockSpec((tm, tn), lambda jj, i: (row(jj, i), col(jj))),
        out_shape=jax.ShapeDtypeStruct((m, n_out), out_dtype),
        scratch_shapes=[pltpu.VMEM((kdim, tn), jnp.bfloat16) for _ in range(2 * n_w)],
        compiler_params=_params(("arbitrary", "arbitrary")),
        name="dense_matmul_" + mode,
    )(*args)


def _expert_kernel(te_ref, nu_ref, *refs, n_w, mode):
    a_ref, w_refs, _, _, _, o_ref, s_refs = _split_refs(refs, n_w, False, False, mode)
    j = pl.program_id(0)
    i = pl.program_id(1)
    fresh = jnp.logical_or(i == 0, te_ref[i] != te_ref[jnp.maximum(i - 1, 0)])

    @pl.when(fresh)
    def _():
        for w_ref, s_ref in zip(w_refs, s_refs):
            s_ref[...] = w_ref[...].astype(jnp.bfloat16)

    @pl.when(i < nu_ref[0])
    def _():
        a = a_ref[...]
        ys = [jnp.dot(a, s_ref[...], preferred_element_type=jnp.float32) for s_ref in s_refs]
        _epilogue(ys, j, o_ref, [], None, None, mode=mode, heads_per_tile=0, n_norm_tiles=0)

    @pl.when(i >= nu_ref[0])
    def _():
        o_ref[...] = jnp.zeros(o_ref.shape, o_ref.dtype)


def _expert_matmul(a, weights, layer, tables, *, n_out, tm, tn, mode="plain",
                   out_dtype=jnp.float32):
    m, kdim = a.shape
    tn = min(tn, n_out)
    nj, ni = n_out // tn, m // tm
    w_spec = pl.BlockSpec((None, None, kdim, tn), lambda j, i, te, nu: (layer, te[i], 0, j))
    return pl.pallas_call(
        functools.partial(_expert_kernel, n_w=len(weights), mode=mode),
        grid_spec=pltpu.PrefetchScalarGridSpec(
            num_scalar_prefetch=2, grid=(nj, ni),
            in_specs=[pl.BlockSpec((tm, kdim), lambda j, i, te, nu: (i, 0))]
            + [w_spec] * len(weights),
            out_specs=pl.BlockSpec((tm, tn), lambda j, i, te, nu: (i, j)),
            scratch_shapes=[pltpu.VMEM((kdim, tn), jnp.bfloat16) for _ in weights]),
        out_shape=jax.ShapeDtypeStruct((m, n_out), out_dtype),
        compiler_params=_params(("arbitrary", "arbitrary")),
        name="expert_matmul_" + mode,
    )(*tables, a, *weights)


def _attn_kernel(q_ref, k0_ref, k1_ref, k2_ref, v0_ref, v1_ref, v2_ref, tab_ref, o_ref,
                 bias_ref, *, heads):
    b = pl.program_id(1)
    i = pl.program_id(2)
    rows = ATTN_Q_ROWS

    @pl.when(jnp.logical_and(b == 0, i == 0))
    def _():
        qi = lax.broadcasted_iota(jnp.int32, (rows, ATTN_KEYS), 0) // CHUNK
        kc = lax.broadcasted_iota(jnp.int32, (rows, ATTN_KEYS), 1) // CHUNK
        in_band = jnp.logical_and(kc >= qi, kc <= qi + N_LEFT_CHUNKS)
        for h in range(heads):
            tab = jnp.broadcast_to(tab_ref[h], (rows, ATTN_TABLE))
            rolled = pltpu.roll(tab, 0, 1, stride=1, stride_axis=0)
            bias_ref[h] = jnp.where(in_band, rolled[:, :ATTN_KEYS], MASK_VALUE)

    k_refs = (k0_ref, k1_ref, k2_ref)
    v_refs = (v0_ref, v1_ref, v2_ref)

    def scores(h):
        sl = slice(h * HEAD_DIM, (h + 1) * HEAD_DIM)
        q = q_ref[:, sl]
        s = []
        for p in range(ATTN_PIECES):
            sp = lax.dot_general(q, k_refs[p][:, sl], (((1,), (1,)), ((), ())),
                                 preferred_element_type=jnp.float32)
            sp = sp + bias_ref[h, :, p * rows:(p + 1) * rows]
            if p < ATTN_PIECES - 1:
                sp = jnp.where(i + p >= ATTN_PIECES - 1, sp, MASK_VALUE)
            s.append(sp)
        return s

    s_next = scores(0)
    for h in range(heads):
        sl = slice(h * HEAD_DIM, (h + 1) * HEAD_DIM)
        s = s_next
        if h + 1 < heads:
            s_next = scores(h + 1)
        mx = jnp.max(s[0], axis=-1, keepdims=True)
        for p in range(1, ATTN_PIECES):
            mx = jnp.maximum(mx, jnp.max(s[p], axis=-1, keepdims=True))
        denom = None
        out = None
        for p in range(ATTN_PIECES):
            e = jnp.exp2(s[p] - mx)
            dsum = jnp.sum(e, axis=-1, keepdims=True)
            pv = jnp.dot(e.astype(jnp.bfloat16), v_refs[p][:, sl],
                         preferred_element_type=jnp.float32)
            denom = dsum if denom is None else denom + dsum
            out = pv if out is None else out + pv
        o_ref[:, sl] = (out / denom).astype(o_ref.dtype)


def _bias_rows(rel_bias):
    c = jnp.arange(ATTN_TABLE)
    dist = (ATTN_PIECES - 1) * ATTN_Q_ROWS - c
    dist = jnp.where(c < ATTN_KEYS, dist, MAX_REL)
    rel = jnp.clip(dist, -MAX_REL, MAX_REL) + MAX_REL
    rows = rel_bias.astype(jnp.float32)[:, rel] * LOG2E
    return rows.reshape(rel_bias.shape[0], 1, ATTN_TABLE)


def _attention(qkv, rel_bias, batch, seq, heads_per_step=8):
    t, d3 = qkv.shape
    d = d3 // 3
    n_heads = d // HEAD_DIM
    hg = min(heads_per_step, n_heads)
    n_groups = n_heads // hg
    rows = ATTN_Q_ROWS
    nblk = seq // rows
    width = hg * HEAD_DIM

    def q_map(g, b, i):
        return (b * nblk + i, g)

    def kv_map(section, back):
        def f(g, b, i):
            return (b * nblk + jnp.maximum(i - back, 0), section * n_groups + g)
        return f

    blk = lambda f: pl.BlockSpec((rows, width), f)
    return pl.pallas_call(
        functools.partial(_attn_kernel, heads=hg),
        grid=(n_groups, batch, nblk),
        in_specs=[blk(q_map),
                  blk(kv_map(1, 2)), blk(kv_map(1, 1)), blk(kv_map(1, 0)),
                  blk(kv_map(2, 2)), blk(kv_map(2, 1)), blk(kv_map(2, 0)),
                  pl.BlockSpec((hg, 1, ATTN_TABLE), lambda g, b, i: (g, 0, 0))],
        out_specs=blk(q_map),
        out_shape=jax.ShapeDtypeStruct((t, d), jnp.bfloat16),
        scratch_shapes=[pltpu.VMEM((hg, rows, ATTN_KEYS), jnp.float32)],
        compiler_params=_params(("arbitrary", "arbitrary", "arbitrary")),
        name="band_attention",
    )(qkv, qkv, qkv, qkv, qkv, qkv, qkv, _bias_rows(rel_bias))


def _conv_kernel(prev_ref, cur_ref, w_ref, bdw_ref, g_ref, b_ref, o_ref, buf_ref, cv_ref,
                 *, col_chunk):
    i = pl.program_id(1)
    d = cur_ref.shape[1]
    halo = CONV_HALO
    buf_ref[0:halo, :] = jnp.where(i > 0, prev_ref[...], 0.0)
    buf_ref[halo:halo + CONV_ROWS, :] = cur_ref[...]
    first = halo - (CONV_WIDTH - 1)
    for c0 in range(0, d, col_chunk):
        cs = slice(c0, c0 + col_chunk)
        acc = jnp.zeros((CONV_ROWS, col_chunk), jnp.float32) + bdw_ref[:, cs]
        for s in range(SUBLANES):
            taps = [t for t in range(CONV_WIDTH) if (first + t) % SUBLANES == s]
            n_rows = CONV_ROWS + (SUBLANES if s else 0)
            part = None
            for t in taps:
                start = first + t - s
                wt = w_ref[t * SUBLANES:(t + 1) * SUBLANES, cs][None]
                term = buf_ref[start:start + n_rows, cs].reshape(-1, SUBLANES, col_chunk) * wt
                part = term if part is None else part + term
            part = part.reshape(n_rows, col_chunk)
            if s:
                part = pltpu.roll(part, n_rows - s, 0)[:CONV_ROWS, :]
            acc = acc + part
        cv_ref[:, cs] = acc
    x = cv_ref[...]
    mu = jnp.mean(x, axis=-1, keepdims=True)
    xc = x - mu
    var = jnp.mean(xc * xc, axis=-1, keepdims=True)
    y = xc * lax.rsqrt(var + LN_EPS) * g_ref[...] + b_ref[...]
    o_ref[...] = (y * jax.nn.sigmoid(y)).astype(o_ref.dtype)


def _conv_ln_swish(u, w_dw, b_dw, ln_g, ln_b, layer, batch, seq):
    t, d = u.shape
    nblk = seq // CONV_ROWS
    ratio = CONV_ROWS // CONV_HALO
    row3 = lambda v: v.reshape(v.shape[0], 1, d)
    vec = pl.BlockSpec((None, 1, d), lambda b, i: (layer, 0, 0))
    return pl.pallas_call(
        functools.partial(_conv_kernel, col_chunk=min(256, d)),
        grid=(batch, nblk),
        in_specs=[pl.BlockSpec((CONV_HALO, d),
                               lambda b, i: (jnp.maximum((b * nblk + i) * ratio - 1, 0), 0)),
                  pl.BlockSpec((CONV_ROWS, d), lambda b, i: (b * nblk + i, 0)),
                  pl.BlockSpec((None, CONV_WIDTH * SUBLANES, d), lambda b, i: (layer, 0, 0)),
                  vec, vec, vec],
        out_specs=pl.BlockSpec((CONV_ROWS, d), lambda b, i: (b * nblk + i, 0)),
        out_shape=jax.ShapeDtypeStruct((t, d), jnp.bfloat16),
        scratch_shapes=[pltpu.VMEM((CONV_HALO + CONV_ROWS, d), jnp.float32),
                        pltpu.VMEM((CONV_ROWS, d), jnp.float32)],
        compiler_params=_params(("parallel", "parallel")),
        name="conv_ln_swish",
    )(u, u, jnp.repeat(w_dw, SUBLANES, axis=1), row3(b_dw), row3(ln_g), row3(ln_b))


def _router_kernel(x_ref, g_ref, whi_ref, wlo_ref, idx_ref, wt_ref, *, n_experts):
    hn = _rms_rows(x_ref[...], g_ref[...])
    hi = hn.astype(jnp.bfloat16)
    lo = (hn - hi.astype(jnp.float32)).astype(jnp.bfloat16)
    dot = functools.partial(jnp.dot, preferred_element_type=jnp.float32)
    logits = dot(hi, whi_ref[...]) + (dot(hi, wlo_ref[...]) + dot(lo, whi_ref[...]))
    lane = lax.broadcasted_iota(jnp.int32, logits.shape, 1)
    logits = jnp.where(lane < n_experts, logits, -jnp.inf)
    v1 = jnp.max(logits, axis=-1, keepdims=True)
    i1 = jnp.min(jnp.where(logits == v1, lane, LANES), axis=-1, keepdims=True)
    rest = jnp.where(lane == i1, -jnp.inf, logits)
    v2 = jnp.max(rest, axis=-1, keepdims=True)
    i2 = jnp.min(jnp.where(rest == v2, lane, LANES), axis=-1, keepdims=True)
    e2 = jnp.exp(v2 - v1)
    denom = 1.0 + e2
    idx_ref[...] = jnp.where(lane == 0, i1, i2)
    wt_ref[...] = jnp.where(lane == 0, 1.0 / denom, e2 / denom)


def _router(h, g, w_router, tm=256):
    t, d = h.shape
    n_experts = w_router.shape[1]
    tm = min(tm, t)
    w_pad = jnp.zeros((d, LANES), jnp.float32).at[:, :n_experts].set(w_router)
    w_hi = w_pad.astype(jnp.bfloat16)
    w_lo = (w_pad - w_hi.astype(jnp.float32)).astype(jnp.bfloat16)
    return pl.pallas_call(
        functools.partial(_router_kernel, n_experts=n_experts),
        grid=(t // tm,),
        in_specs=[pl.BlockSpec((tm, d), lambda i: (i, 0)),
                  pl.BlockSpec((1, d), lambda i: (0, 0)),
                  pl.BlockSpec((d, LANES), lambda i: (0, 0)),
                  pl.BlockSpec((d, LANES), lambda i: (0, 0))],
        out_specs=[pl.BlockSpec((tm, LANES), lambda i: (i, 0)),
                   pl.BlockSpec((tm, LANES), lambda i: (i, 0))],
        out_shape=[jax.ShapeDtypeStruct((t, LANES), jnp.int32),
                   jax.ShapeDtypeStruct((t, LANES), jnp.float32)],
        compiler_params=_params(("parallel",)),
        name="router",
    )(h, g.reshape(1, d), w_hi, w_lo)


def _row_copy(src_hbm, row, dst, slot, sem):
    return pltpu.make_async_copy(src_hbm.at[pl.ds(row, 1), :], dst.at[pl.ds(slot, 1), :], sem)


def _dispatch_kernel(src_ref, nu_ref, h_hbm, g_ref, o_ref, buf_ref, sem, *, rows, tiles_per_mm):
    step = pl.program_id(0)
    base = step * rows
    used = step < nu_ref[0] * tiles_per_mm

    @pl.when(used)
    def _():
        def issue(r, carry):
            _row_copy(h_hbm, src_ref[base + r], buf_ref, r, sem).start()
            return carry

        lax.fori_loop(0, rows, issue, 0)
        pltpu.make_async_copy(h_hbm.at[pl.ds(0, rows), :], buf_ref, sem).wait()
        o_ref[...] = _rms_rows(buf_ref[...], g_ref[...]).astype(o_ref.dtype)

    @pl.when(jnp.logical_not(used))
    def _():
        o_ref[...] = jnp.zeros(o_ref.shape, o_ref.dtype)


def _dispatch(h, g, src, n_used, tm, rows=256):
    t, d = h.shape
    p = src.shape[0]
    rows = min(rows, tm)
    return pl.pallas_call(
        functools.partial(_dispatch_kernel, rows=rows, tiles_per_mm=tm // rows),
        grid_spec=pltpu.PrefetchScalarGridSpec(
            num_scalar_prefetch=2,
            grid=(p // rows,),
            in_specs=[pl.BlockSpec(memory_space=pl.ANY),
                      pl.BlockSpec((1, d), lambda i, s, nu: (0, 0))],
            out_specs=pl.BlockSpec((rows, d), lambda i, s, nu: (i, 0)),
            scratch_shapes=[pltpu.VMEM((rows, d), jnp.float32),
                            pltpu.SemaphoreType.DMA(())]),
        out_shape=jax.ShapeDtypeStruct((p, d), jnp.bfloat16),
        compiler_params=_params(("arbitrary",)),
        name="moe_dispatch",
    )(src, n_used, h, g.reshape(1, d))


def _combine_kernel(pos_ref, ys_hbm, h_ref, wt_ref, o_ref, buf_ref, sem, *, rows):
    base = pl.program_id(0) * rows

    def issue(r, carry):
        for k in range(TOP_K):
            _row_copy(ys_hbm, pos_ref[TOP_K * (base + r) + k], buf_ref.at[k], r, sem).start()
        return carry

    lax.fori_loop(0, rows, issue, 0)
    for k in range(TOP_K):
        pltpu.make_async_copy(ys_hbm.at[pl.ds(0, rows), :], buf_ref.at[k], sem).wait()
    wt = wt_ref[...]
    acc = h_ref[...]
    for k in range(TOP_K):
        acc = acc + wt[:, k:k + 1] * buf_ref[k]
    o_ref[...] = acc


def _combine(h, ys, pos, wts, rows=128):
    t, d = h.shape
    rows = min(rows, t)
    return pl.pallas_call(
        functools.partial(_combine_kernel, rows=rows),
        grid_spec=pltpu.PrefetchScalarGridSpec(
            num_scalar_prefetch=1,
            grid=(t // rows,),
            in_specs=[pl.BlockSpec(memory_space=pl.ANY),
                      pl.BlockSpec((rows, d), lambda i, s: (i, 0)),
                      pl.BlockSpec((rows, LANES), lambda i, s: (i, 0))],
            out_specs=pl.BlockSpec((rows, d), lambda i, s: (i, 0)),
            scratch_shapes=[pltpu.VMEM((TOP_K, rows, d), jnp.float32),
                            pltpu.SemaphoreType.DMA(())]),
        out_shape=jax.ShapeDtypeStruct((t, d), jnp.float32),
        compiler_params=_params(("arbitrary",)),
        name="moe_combine",
    )(pos, ys, h, wts)


def _route_tables(idx, n_experts, tm):
    t = idx.shape[0]
    n_assign = t * TOP_K
    p = n_assign + n_experts * tm
    flat = idx.reshape(-1)
    onehot = (flat[:, None] == jnp.arange(n_experts)[None, :]).astype(jnp.int32)
    csum = jnp.cumsum(onehot, axis=0)
    counts = csum[-1]
    rank = jnp.sum(csum * onehot, axis=1) - 1
    padded = ((counts + tm - 1) // tm) * tm
    ends = jnp.cumsum(padded)
    starts = ends - padded
    pos = jnp.sum(starts[None, :] * onehot, axis=1) + rank
    src = jnp.zeros((p,), jnp.int32).at[pos].set(jnp.arange(n_assign, dtype=jnp.int32) // TOP_K)
    tile_start = jnp.arange(p // tm, dtype=jnp.int32) * tm
    tile_expert = jnp.sum((tile_start[:, None] >= ends[None, :]).astype(jnp.int32), axis=1)
    tile_expert = jnp.minimum(tile_expert, n_experts - 1)
    n_used = (ends[-1:] // tm).astype(jnp.int32)
    return pos.astype(jnp.int32), src, tile_expert.astype(jnp.int32), n_used


def _moe(h, g, w_router, w_gate, w_up, w_down, layer, tm=512):
    n_experts = w_router.shape[1]
    f = w_gate.shape[3]
    d = h.shape[1]
    idx_pad, wts = _router(h, g, w_router)
    pos, src, tile_expert, n_used = _route_tables(idx_pad[:, :TOP_K], n_experts, tm)
    xs = _dispatch(h, g, src, n_used, tm)
    tables = (tile_expert, n_used)
    act = _expert_matmul(xs, [w_gate, w_up], layer, tables, n_out=f, tm=tm, tn=512,
                         mode="silu_mul", out_dtype=jnp.bfloat16)
    ys = _expert_matmul(act, [w_down], layer, tables, n_out=d, tm=tm, tn=1024)
    return _combine(h, ys, pos, wts)


def kernel(x, norm_mix, norm_ffn, attn_w_qkv, attn_q_norm, attn_k_norm, attn_rel_bias,
           attn_w_out, conv_w_in, conv_b_in, conv_w_dw, conv_b_dw, conv_ln_g, conv_ln_b,
           conv_w_out, conv_b_out, ffn_w_gate_up, ffn_w_down, moe_w_router, moe_w_gate,
           moe_w_up, moe_w_down):
    batch, seq, d = x.shape
    depth = norm_mix.shape[0]
    d_ff = ffn_w_gate_up.shape[2] // 2
    bf16 = jnp.bfloat16
    row3 = lambda v: v.reshape(v.shape[0], 1, v.shape[1])
    conv_b_in3, conv_b_out3 = row3(conv_b_in), row3(conv_b_out)
    h = x.reshape(batch * seq, d)
    for i in range(depth):
        j = i // 2
        hn = _rmsnorm(h, norm_mix[i])
        if i % 2 == 0:
            gains = jnp.stack([attn_q_norm[j] * (HEAD_DIM ** -0.5 * LOG2E),
                               attn_k_norm[j]]).reshape(2, 1, HEAD_DIM)
            qkv = _dense_matmul(hn, [(attn_w_qkv, 0)], j, n_out=3 * d, tm=1024, tn=min(1024, d),
                                mode="headnorm", gains=gains, out_dtype=bf16)
            o = _attention(qkv, attn_rel_bias[j], batch, seq)
            h = _dense_matmul(o, [(attn_w_out, 0)], j, n_out=d, tm=1024, tn=512, res=h)
            hn = _rmsnorm(h, norm_ffn[i])
            act = _dense_matmul(hn, [(ffn_w_gate_up, 0), (ffn_w_gate_up, d_ff)], j, n_out=d_ff,
                                tm=1024, tn=512, mode="silu_mul", out_dtype=bf16)
            h = _dense_matmul(act, [(ffn_w_down, 0)], j, n_out=d, tm=512, tn=512, res=h)
        else:
            u = _dense_matmul(hn, [(conv_w_in, 0), (conv_w_in, d)], j, n_out=d, tm=1024,
                              tn=512, mode="mul_sig",
                              biases=[(conv_b_in3, 0), (conv_b_in3, d)])
            c = _conv_ln_swish(u, conv_w_dw, conv_b_dw, conv_ln_g, conv_ln_b, j, batch, seq)
            h = _dense_matmul(c, [(conv_w_out, 0)], j, n_out=d, tm=1024, tn=512,
                              biases=[(conv_b_out3, 0)], res=h)
            h = _moe(h, norm_ffn[i], moe_w_router[j], moe_w_gate, moe_w_up, moe_w_down, j)
    return h.reshape(batch, seq, d)
```

```python
import functools
import math

import jax
import jax.numpy as jnp
from jax import lax
from jax.experimental import pallas as pl
from jax.experimental.pallas import tpu as pltpu

CHUNK = 64
N_LEFT_CHUNKS = 8
HEAD_DIM = 128
MAX_REL = 128
CONV_WIDTH = 31
TOP_K = 2
RMS_EPS = 1e-6
LN_EPS = 1e-5

LANES = 128
SUBLANES = 8
V7X_VMEM_LIMIT = 56 * 1024 * 1024
MASK_VALUE = -1e30
LOG2E = math.log2(math.e)

ATTN_Q_ROWS = 4 * CHUNK
ATTN_PIECES = 3
ATTN_KEYS = ATTN_PIECES * ATTN_Q_ROWS
ATTN_TABLE = ATTN_KEYS + ATTN_Q_ROWS
CONV_ROWS = 64
CONV_HALO = 32


def _params(sem, vmem=V7X_VMEM_LIMIT):
    return pltpu.CompilerParams(dimension_semantics=sem, vmem_limit_bytes=vmem)


def _rms_rows(x, g):
    ms = jnp.mean(x * x, axis=-1, keepdims=True)
    return x * lax.rsqrt(ms + RMS_EPS) * g


def _rmsnorm_kernel(x_ref, g_ref, o_ref):
    o_ref[...] = _rms_rows(x_ref[...], g_ref[...]).astype(o_ref.dtype)


def _rmsnorm(x, g, tm=256):
    m, d = x.shape
    tm = min(tm, m)
    return pl.pallas_call(
        _rmsnorm_kernel,
        grid=(m // tm,),
        in_specs=[pl.BlockSpec((tm, d), lambda i: (i, 0)),
                  pl.BlockSpec((1, d), lambda i: (0, 0))],
        out_specs=pl.BlockSpec((tm, d), lambda i: (i, 0)),
        out_shape=jax.ShapeDtypeStruct((m, d), jnp.bfloat16),
        compiler_params=_params(("parallel",)),
        name="rmsnorm",
    )(x, g.reshape(1, d))


def _epilogue(ys, j, o_ref, b_refs, res_ref, gain_ref, *, mode, heads_per_tile, n_norm_tiles):
    if b_refs:
        ys = [y + b_ref[...] for y, b_ref in zip(ys, b_refs)]
    if mode == "plain":
        y = ys[0]
        if res_ref is not None:
            y = y + res_ref[...]
        o_ref[...] = y.astype(o_ref.dtype)
    elif mode == "silu_mul":
        o_ref[...] = (ys[0] * jax.nn.sigmoid(ys[0]) * ys[1]).astype(o_ref.dtype)
    elif mode == "mul_sig":
        o_ref[...] = (ys[0] * jax.nn.sigmoid(ys[1])).astype(o_ref.dtype)
    else:
        y = ys[0]

        @pl.when(j < n_norm_tiles)
        def _():
            gain = gain_ref[...]
            for h in range(heads_per_tile):
                sl = slice(h * HEAD_DIM, (h + 1) * HEAD_DIM)
                o_ref[:, sl] = _rms_rows(y[:, sl], gain).astype(o_ref.dtype)

        @pl.when(j >= n_norm_tiles)
        def _():
            o_ref[...] = y.astype(o_ref.dtype)


def _split_refs(refs, n_w, has_bias, has_res, mode):
    refs = list(refs)
    a_ref = refs.pop(0)
    w_refs = [refs.pop(0) for _ in range(n_w)]
    b_refs = [refs.pop(0) for _ in range(n_w)] if has_bias else []
    res_ref = refs.pop(0) if has_res else None
    gain_ref = refs.pop(0) if mode == "headnorm" else None
    o_ref = refs.pop(0)
    return a_ref, w_refs, b_refs, res_ref, gain_ref, o_ref, refs


def _dense_kernel(*refs, n_w, mode, has_bias, has_res, nj, kc, **epi):
    a_ref, w_refs, b_refs, res_ref, gain_ref, o_ref, s_refs = _split_refs(
        refs, n_w, has_bias, has_res, mode)
    jj = pl.program_id(0)
    i = pl.program_id(1)
    rows = pl.ds(pl.multiple_of(i * kc, kc), kc)

    def convert(parity):
        for t, w_ref in enumerate(w_refs):
            s_refs[2 * t + parity][rows, :] = w_ref[...].astype(jnp.bfloat16)

    def multiply(parity):
        a = a_ref[...]
        ys = [jnp.dot(a, s_refs[2 * t + parity][...], preferred_element_type=jnp.float32)
              for t in range(n_w)]
        _epilogue(ys, jj - 1, o_ref, b_refs, res_ref, gain_ref, mode=mode, **epi)

    @pl.when(jj == 0)
    def _():
        convert(0)

    for parity in range(2):
        @pl.when(jnp.logical_and(jj > 0, jj % 2 == parity))
        def _():
            convert(parity)
            multiply(1 - parity)


def _dense_matmul(a, weights, layer, *, n_out, tm, tn, mode="plain", biases=None, res=None,
                  gains=None, out_dtype=jnp.float32):
    m, kdim = a.shape
    tm, tn = min(tm, m), min(tn, n_out)
    nj, ni = n_out // tn, m // tm
    kc = kdim // ni
    assert kc * ni == kdim and kc % 16 == 0, (kdim, ni)
    n_w = len(weights)
    col = lambda jj: jnp.maximum(jj - 1, 0)
    row = lambda jj, i: jnp.where(jj == 0, 0, i)

    in_specs = [pl.BlockSpec((tm, kdim), lambda jj, i: (row(jj, i), 0))]
    args = [a]
    for w, col0 in weights:
        off = col0 // tn
        in_specs.append(pl.BlockSpec(
            (None, kc, tn), lambda jj, i, off=off: (layer, i, jnp.minimum(jj, nj - 1) + off)))
        args.append(w)
    if biases is not None:
        for b, col0 in biases:
            off = col0 // tn
            in_specs.append(pl.BlockSpec((None, 1, tn),
                                         lambda jj, i, off=off: (layer, 0, col(jj) + off)))
            args.append(b)
    if res is not None:
        in_specs.append(pl.BlockSpec((tm, tn), lambda jj, i: (row(jj, i), col(jj))))
        args.append(res)
    heads_per_tile = n_norm_tiles = 0
    if mode == "headnorm":
        heads_per_tile = tn // HEAD_DIM
        tiles_per_section = (n_out // 3) // tn
        n_norm_tiles = 2 * tiles_per_section
        in_specs.append(pl.BlockSpec(
            (None, 1, HEAD_DIM),
            lambda jj, i: (jnp.minimum(col(jj) // tiles_per_section, 1), 0, 0)))
        args.append(gains)
    body = functools.partial(_dense_kernel, n_w=n_w, mode=mode, has_bias=biases is not None,
                             has_res=res is not None, nj=nj, kc=kc,
                             heads_per_tile=heads_per_tile, n_norm_tiles=n_norm_tiles)
    return pl.pallas_call(
        body, grid=(nj + 1, ni), in_specs=in_specs,
        out_specs=pl.BlockSpec((tm, tn), lambda jj, i: (row(jj, i), col(jj))),
        out_shape=jax.ShapeDtypeStruct((m, n_out), out_dtype),
        scratch_shapes=[pltpu.VMEM((kdim, tn), jnp.bfloat16) for _ in range(2 * n_w)],
        compiler_params=_params(("arbitrary", "arbitrary")),
        name="dense_matmul_" + mode,
    )(*args)


def _expert_kernel(te_ref, nu_ref, *refs, n_w, mode):
    a_ref, w_refs, _, _, _, o_ref, s_refs = _split_refs(refs, n_w, False, False, mode)
    j = pl.program_id(0)
    i = pl.program_id(1)
    fresh = jnp.logical_or(i == 0, te_ref[i] != te_ref[jnp.maximum(i - 1, 0)])

    @pl.when(fresh)
    def _():
        for w_ref, s_ref in zip(w_refs, s_refs):
            s_ref[...] = w_ref[...].astype(jnp.bfloat16)

    @pl.when(i < nu_ref[0])
    def _():
        a = a_ref[...]
        ys = [jnp.dot(a, s_ref[...], preferred_element_type=jnp.float32) for s_ref in s_refs]
        if mode == "packed":
            o_ref[...] = _pack_halves(ys[0])
        else:
            _epilogue(ys, j, o_ref, [], None, None, mode=mode, heads_per_tile=0,
                      n_norm_tiles=0)

    @pl.when(i >= nu_ref[0])
    def _():
        o_ref[...] = jnp.zeros(o_ref.shape, o_ref.dtype)


def _expert_matmul(a, weights, layer, tables, *, n_out, tm, tn, mode="plain",
                   out_dtype=jnp.float32):
    m, kdim = a.shape
    tn = min(tn, n_out)
    nj, ni = n_out // tn, m // tm
    w_spec = pl.BlockSpec((None, None, kdim, tn), lambda j, i, te, nu: (layer, te[i], 0, j))
    if mode == "packed":
        out_spec = pl.BlockSpec((tm, tn // 2), lambda j, i, te, nu: (i, j))
        out_shape = jax.ShapeDtypeStruct((m, n_out // 2), jnp.uint32)
    else:
        out_spec = pl.BlockSpec((tm, tn), lambda j, i, te, nu: (i, j))
        out_shape = jax.ShapeDtypeStruct((m, n_out), out_dtype)
    return pl.pallas_call(
        functools.partial(_expert_kernel, n_w=len(weights), mode=mode),
        grid_spec=pltpu.PrefetchScalarGridSpec(
            num_scalar_prefetch=2, grid=(nj, ni),
            in_specs=[pl.BlockSpec((tm, kdim), lambda j, i, te, nu: (i, 0))]
            + [w_spec] * len(weights),
            out_specs=out_spec,
            scratch_shapes=[pltpu.VMEM((kdim, tn), jnp.bfloat16) for _ in weights]),
        out_shape=out_shape,
        compiler_params=_params(("arbitrary", "arbitrary")),
        name="expert_matmul_" + mode,
    )(*tables, a, *weights)


def _attn_kernel(q_ref, k0_ref, k1_ref, k2_ref, v0_ref, v1_ref, v2_ref, tab_ref, o_ref,
                 bias_ref, *, heads):
    b = pl.program_id(1)
    i = pl.program_id(2)
    rows = ATTN_Q_ROWS

    @pl.when(jnp.logical_and(b == 0, i == 0))
    def _():
        qi = lax.broadcasted_iota(jnp.int32, (rows, ATTN_KEYS), 0) // CHUNK
        kc = lax.broadcasted_iota(jnp.int32, (rows, ATTN_KEYS), 1) // CHUNK
        in_band = jnp.logical_and(kc >= qi, kc <= qi + N_LEFT_CHUNKS)
        for h in range(heads):
            tab = jnp.broadcast_to(tab_ref[h], (rows, ATTN_TABLE))
            rolled = pltpu.roll(tab, 0, 1, stride=1, stride_axis=0)
            bias_ref[h] = jnp.where(in_band, rolled[:, :ATTN_KEYS], MASK_VALUE)

    k_refs = (k0_ref, k1_ref, k2_ref)
    v_refs = (v0_ref, v1_ref, v2_ref)

    def scores(h):
        sl = slice(h * HEAD_DIM, (h + 1) * HEAD_DIM)
        q = q_ref[:, sl]
        s = []
        for p in range(ATTN_PIECES):
            sp = lax.dot_general(q, k_refs[p][:, sl], (((1,), (1,)), ((), ())),
                                 preferred_element_type=jnp.float32)
            sp = sp + bias_ref[h, :, p * rows:(p + 1) * rows]
            if p < ATTN_PIECES - 1:
                sp = jnp.where(i + p >= ATTN_PIECES - 1, sp, MASK_VALUE)
            s.append(sp)
        return s

    s_next = scores(0)
    for h in range(heads):
        sl = slice(h * HEAD_DIM, (h + 1) * HEAD_DIM)
        s = s_next
        if h + 1 < heads:
            s_next = scores(h + 1)
        mx = jnp.max(s[0], axis=-1, keepdims=True)
        for p in range(1, ATTN_PIECES):
            mx = jnp.maximum(mx, jnp.max(s[p], axis=-1, keepdims=True))
        denom = None
        out = None
        for p in range(ATTN_PIECES):
            e = jnp.exp2(s[p] - mx)
            dsum = jnp.sum(e, axis=-1, keepdims=True)
            pv = jnp.dot(e.astype(jnp.bfloat16), v_refs[p][:, sl],
                         preferred_element_type=jnp.float32)
            denom = dsum if denom is None else denom + dsum
            out = pv if out is None else out + pv
        o_ref[:, sl] = (out / denom).astype(o_ref.dtype)


def _bias_rows(rel_bias):
    c = jnp.arange(ATTN_TABLE)
    dist = (ATTN_PIECES - 1) * ATTN_Q_ROWS - c
    dist = jnp.where(c < ATTN_KEYS, dist, MAX_REL)
    rel = jnp.clip(dist, -MAX_REL, MAX_REL) + MAX_REL
    rows = rel_bias.astype(jnp.float32)[:, rel] * LOG2E
    return rows.reshape(rel_bias.shape[0], 1, ATTN_TABLE)


def _attention(qkv, rel_bias, batch, seq, heads_per_step=8):
    t, d3 = qkv.shape
    d = d3 // 3
    n_heads = d // HEAD_DIM
    hg = min(heads_per_step, n_heads)
    n_groups = n_heads // hg
    rows = ATTN_Q_ROWS
    nblk = seq // rows
    width = hg * HEAD_DIM

    def q_map(g, b, i):
        return (b * nblk + i, g)

    def kv_map(section, back):
        def f(g, b, i):
            return (b * nblk + jnp.maximum(i - back, 0), section * n_groups + g)
        return f

    blk = lambda f: pl.BlockSpec((rows, width), f)
    return pl.pallas_call(
        functools.partial(_attn_kernel, heads=hg),
        grid=(n_groups, batch, nblk),
        in_specs=[blk(q_map),
                  blk(kv_map(1, 2)), blk(kv_map(1, 1)), blk(kv_map(1, 0)),
                  blk(kv_map(2, 2)), blk(kv_map(2, 1)), blk(kv_map(2, 0)),
                  pl.BlockSpec((hg, 1, ATTN_TABLE), lambda g, b, i: (g, 0, 0))],
        out_specs=blk(q_map),
        out_shape=jax.ShapeDtypeStruct((t, d), jnp.bfloat16),
        scratch_shapes=[pltpu.VMEM((hg, rows, ATTN_KEYS), jnp.float32)],
        compiler_params=_params(("arbitrary", "arbitrary", "arbitrary")),
        name="band_attention",
    )(qkv, qkv, qkv, qkv, qkv, qkv, qkv, _bias_rows(rel_bias))


def _conv_kernel(prev_ref, cur_ref, w_ref, bdw_ref, g_ref, b_ref, o_ref, buf_ref, cv_ref,
                 *, col_chunk):
    i = pl.program_id(1)
    d = cur_ref.shape[1]
    halo = CONV_HALO
    buf_ref[0:halo, :] = jnp.where(i > 0, prev_ref[...], 0.0)
    buf_ref[halo:halo + CONV_ROWS, :] = cur_ref[...]
    first = halo - (CONV_WIDTH - 1)
    for c0 in range(0, d, col_chunk):
        cs = slice(c0, c0 + col_chunk)
        acc = jnp.zeros((CONV_ROWS, col_chunk), jnp.float32) + bdw_ref[:, cs]
        for s in range(SUBLANES):
            taps = [t for t in range(CONV_WIDTH) if (first + t) % SUBLANES == s]
            n_rows = CONV_ROWS + (SUBLANES if s else 0)
            part = None
            for t in taps:
                start = first + t - s
                wt = w_ref[t * SUBLANES:(t + 1) * SUBLANES, cs][None]
                term = buf_ref[start:start + n_rows, cs].reshape(-1, SUBLANES, col_chunk) * wt
                part = term if part is None else part + term
            part = part.reshape(n_rows, col_chunk)
            if s:
                part = pltpu.roll(part, n_rows - s, 0)[:CONV_ROWS, :]
            acc = acc + part
        cv_ref[:, cs] = acc
    x = cv_ref[...]
    mu = jnp.mean(x, axis=-1, keepdims=True)
    xc = x - mu
    var = jnp.mean(xc * xc, axis=-1, keepdims=True)
    y = xc * lax.rsqrt(var + LN_EPS) * g_ref[...] + b_ref[...]
    o_ref[...] = (y * jax.nn.sigmoid(y)).astype(o_ref.dtype)


def _conv_ln_swish(u, w_dw, b_dw, ln_g, ln_b, layer, batch, seq):
    t, d = u.shape
    nblk = seq // CONV_ROWS
    ratio = CONV_ROWS // CONV_HALO
    row3 = lambda v: v.reshape(v.shape[0], 1, d)
    vec = pl.BlockSpec((None, 1, d), lambda b, i: (layer, 0, 0))
    return pl.pallas_call(
        functools.partial(_conv_kernel, col_chunk=min(256, d)),
        grid=(batch, nblk),
        in_specs=[pl.BlockSpec((CONV_HALO, d),
                               lambda b, i: (jnp.maximum((b * nblk + i) * ratio - 1, 0), 0)),
                  pl.BlockSpec((CONV_ROWS, d), lambda b, i: (b * nblk + i, 0)),
                  pl.BlockSpec((None, CONV_WIDTH * SUBLANES, d), lambda b, i: (layer, 0, 0)),
                  vec, vec, vec],
        out_specs=pl.BlockSpec((CONV_ROWS, d), lambda b, i: (b * nblk + i, 0)),
        out_shape=jax.ShapeDtypeStruct((t, d), jnp.bfloat16),
        scratch_shapes=[pltpu.VMEM((CONV_HALO + CONV_ROWS, d), jnp.float32),
                        pltpu.VMEM((CONV_ROWS, d), jnp.float32)],
        compiler_params=_params(("parallel", "parallel")),
        name="conv_ln_swish",
    )(u, u, jnp.repeat(w_dw, SUBLANES, axis=1), row3(b_dw), row3(ln_g), row3(ln_b))


def _pack_halves(x):
    half = x.shape[1] // 2
    bits = pltpu.bitcast(x.astype(jnp.bfloat16).astype(jnp.float32), jnp.uint32)
    return (bits[:, :half] >> 16) | bits[:, half:]


def _unpack_halves(words):
    lo = pltpu.bitcast(words << 16, jnp.float32)
    hi = pltpu.bitcast(words & jnp.uint32(0xFFFF0000), jnp.float32)
    return lo, hi


def _router_kernel(x_ref, g_ref, whi_ref, wlo_ref, idx_ref, wt_ref, xp_ref, *, n_experts):
    hn = _rms_rows(x_ref[...], g_ref[...])
    xp_ref[...] = _pack_halves(hn)
    hi = hn.astype(jnp.bfloat16)
    lo = (hn - hi.astype(jnp.float32)).astype(jnp.bfloat16)
    dot = functools.partial(jnp.dot, preferred_element_type=jnp.float32)
    logits = dot(hi, whi_ref[...]) + (dot(hi, wlo_ref[...]) + dot(lo, whi_ref[...]))
    lane = lax.broadcasted_iota(jnp.int32, logits.shape, 1)
    logits = jnp.where(lane < n_experts, logits, -jnp.inf)
    v1 = jnp.max(logits, axis=-1, keepdims=True)
    i1 = jnp.min(jnp.where(logits == v1, lane, LANES), axis=-1, keepdims=True)
    rest = jnp.where(lane == i1, -jnp.inf, logits)
    v2 = jnp.max(rest, axis=-1, keepdims=True)
    i2 = jnp.min(jnp.where(rest == v2, lane, LANES), axis=-1, keepdims=True)
    e2 = jnp.exp(v2 - v1)
    denom = 1.0 + e2
    idx_ref[...] = jnp.where(lane == 0, i1, i2)
    wt_ref[...] = jnp.where(lane == 0, 1.0 / denom, e2 / denom)


def _router(h, g, w_router, tm=256):
    t, d = h.shape
    n_experts = w_router.shape[1]
    tm = min(tm, t)
    w_pad = jnp.zeros((d, LANES), jnp.float32).at[:, :n_experts].set(w_router)
    w_hi = w_pad.astype(jnp.bfloat16)
    w_lo = (w_pad - w_hi.astype(jnp.float32)).astype(jnp.bfloat16)
    return pl.pallas_call(
        functools.partial(_router_kernel, n_experts=n_experts),
        grid=(t // tm,),
        in_specs=[pl.BlockSpec((tm, d), lambda i: (i, 0)),
                  pl.BlockSpec((1, d), lambda i: (0, 0)),
                  pl.BlockSpec((d, LANES), lambda i: (0, 0)),
                  pl.BlockSpec((d, LANES), lambda i: (0, 0))],
        out_specs=[pl.BlockSpec((tm, LANES), lambda i: (i, 0)),
                   pl.BlockSpec((tm, LANES), lambda i: (i, 0)),
                   pl.BlockSpec((tm, d // 2), lambda i: (i, 0))],
        out_shape=[jax.ShapeDtypeStruct((t, LANES), jnp.int32),
                   jax.ShapeDtypeStruct((t, LANES), jnp.float32),
                   jax.ShapeDtypeStruct((t, d // 2), jnp.uint32)],
        compiler_params=_params(("parallel",)),
        name="router",
    )(h, g.reshape(1, d), w_hi, w_lo)


def _row_copy(src_hbm, row, dst, slot, sem):
    return pltpu.make_async_copy(src_hbm.at[pl.ds(row, 1), :], dst.at[pl.ds(slot, 1), :], sem)


def _dispatch_kernel(src_ref, nu_ref, xp_hbm, o_ref, buf_ref, sem, *, rows, tiles_per_mm):
    step = pl.program_id(0)
    base = step * rows
    used = step < nu_ref[0] * tiles_per_mm
    half = buf_ref.shape[1]

    @pl.when(used)
    def _():
        def issue(r, carry):
            _row_copy(xp_hbm, src_ref[base + r], buf_ref, r, sem).start()
            return carry

        lax.fori_loop(0, rows, issue, 0)
        pltpu.make_async_copy(xp_hbm.at[pl.ds(0, rows), :], buf_ref, sem).wait()
        lo, hi = _unpack_halves(buf_ref[...])
        o_ref[:, :half] = lo.astype(o_ref.dtype)
        o_ref[:, half:] = hi.astype(o_ref.dtype)

    @pl.when(jnp.logical_not(used))
    def _():
        o_ref[...] = jnp.zeros(o_ref.shape, o_ref.dtype)


def _dispatch(xp, src, n_used, tm, rows=256):
    t, half = xp.shape
    p = src.shape[0]
    rows = min(rows, tm)
    return pl.pallas_call(
        functools.partial(_dispatch_kernel, rows=rows, tiles_per_mm=tm // rows),
        grid_spec=pltpu.PrefetchScalarGridSpec(
            num_scalar_prefetch=2,
            grid=(p // rows,),
            in_specs=[pl.BlockSpec(memory_space=pl.ANY)],
            out_specs=pl.BlockSpec((rows, 2 * half), lambda i, s, nu: (i, 0)),
            scratch_shapes=[pltpu.VMEM((rows, half), jnp.uint32),
                            pltpu.SemaphoreType.DMA(())]),
        out_shape=jax.ShapeDtypeStruct((p, 2 * half), jnp.bfloat16),
        compiler_params=_params(("arbitrary",)),
        name="moe_dispatch",
    )(src, n_used, xp)


def _combine_kernel(pos_ref, ys_hbm, h_ref, wt_ref, o_ref, buf_ref, sem, *, rows, tn):
    base = pl.program_id(0) * rows

    def issue(r, carry):
        for k in range(TOP_K):
            _row_copy(ys_hbm, pos_ref[TOP_K * (base + r) + k], buf_ref.at[k], r, sem).start()
        return carry

    lax.fori_loop(0, rows, issue, 0)
    for k in range(TOP_K):
        pltpu.make_async_copy(ys_hbm.at[pl.ds(0, rows), :], buf_ref.at[k], sem).wait()
    wt = wt_ref[...]
    half = tn // 2
    for jt in range(o_ref.shape[1] // tn):
        lo_cols = slice(jt * tn, jt * tn + half)
        hi_cols = slice(jt * tn + half, (jt + 1) * tn)
        acc_lo = h_ref[:, lo_cols]
        acc_hi = h_ref[:, hi_cols]
        for k in range(TOP_K):
            lo, hi = _unpack_halves(buf_ref[k, :, jt * half:(jt + 1) * half])
            acc_lo = acc_lo + wt[:, k:k + 1] * lo
            acc_hi = acc_hi + wt[:, k:k + 1] * hi
        o_ref[:, lo_cols] = acc_lo
        o_ref[:, hi_cols] = acc_hi


def _combine(h, ys, pos, wts, tn, rows=128):
    t, d = h.shape
    rows = min(rows, t)
    return pl.pallas_call(
        functools.partial(_combine_kernel, rows=rows, tn=tn),
        grid_spec=pltpu.PrefetchScalarGridSpec(
            num_scalar_prefetch=1,
            grid=(t // rows,),
            in_specs=[pl.BlockSpec(memory_space=pl.ANY),
                      pl.BlockSpec((rows, d), lambda i, s: (i, 0)),
                      pl.BlockSpec((rows, LANES), lambda i, s: (i, 0))],
            out_specs=pl.BlockSpec((rows, d), lambda i, s: (i, 0)),
            scratch_shapes=[pltpu.VMEM((TOP_K, rows, d // 2), jnp.uint32),
                            pltpu.SemaphoreType.DMA(())]),
        out_shape=jax.ShapeDtypeStruct((t, d), jnp.float32),
        compiler_params=_params(("arbitrary",)),
        name="moe_combine",
    )(pos, ys, h, wts)


def _route_tables(idx, n_experts, tm):
    t = idx.shape[0]
    n_assign = t * TOP_K
    p = n_assign + n_experts * tm
    flat = idx.reshape(-1)
    onehot = (flat[:, None] == jnp.arange(n_experts)[None, :]).astype(jnp.int32)
    csum = jnp.cumsum(onehot, axis=0)
    counts = csum[-1]
    rank = jnp.sum(csum * onehot, axis=1) - 1
    padded = ((counts + tm - 1) // tm) * tm
    ends = jnp.cumsum(padded)
    starts = ends - padded
    pos = jnp.sum(starts[None, :] * onehot, axis=1) + rank
    src = jnp.zeros((p,), jnp.int32).at[pos].set(jnp.arange(n_assign, dtype=jnp.int32) // TOP_K)
    tile_start = jnp.arange(p // tm, dtype=jnp.int32) * tm
    tile_expert = jnp.sum((tile_start[:, None] >= ends[None, :]).astype(jnp.int32), axis=1)
    tile_expert = jnp.minimum(tile_expert, n_experts - 1)
    n_used = (ends[-1:] // tm).astype(jnp.int32)
    return pos.astype(jnp.int32), src, tile_expert.astype(jnp.int32), n_used


def _moe(h, g, w_router, w_gate, w_up, w_down, layer, tm=512):
    n_experts = w_router.shape[1]
    f = w_gate.shape[3]
    d = h.shape[1]
    idx_pad, wts, xp = _router(h, g, w_router)
    pos, src, tile_expert, n_used = _route_tables(idx_pad[:, :TOP_K], n_experts, tm)
    xs = _dispatch(xp, src, n_used, tm)
    tables = (tile_expert, n_used)
    act = _expert_matmul(xs, [w_gate, w_up], layer, tables, n_out=f, tm=tm, tn=512,
                         mode="silu_mul", out_dtype=jnp.bfloat16)
    tn_down = min(1024, d)
    ys = _expert_matmul(act, [w_down], layer, tables, n_out=d, tm=tm, tn=tn_down,
                        mode="packed")
    return _combine(h, ys, pos, wts, tn_down)


def kernel(x, norm_mix, norm_ffn, attn_w_qkv, attn_q_norm, attn_k_norm, attn_rel_bias,
           attn_w_out, conv_w_in, conv_b_in, conv_w_dw, conv_b_dw, conv_ln_g, conv_ln_b,
           conv_w_out, conv_b_out, ffn_w_gate_up, ffn_w_down, moe_w_router, moe_w_gate,
           moe_w_up, moe_w_down):
    batch, seq, d = x.shape
    depth = norm_mix.shape[0]
    d_ff = ffn_w_gate_up.shape[2] // 2
    bf16 = jnp.bfloat16
    row3 = lambda v: v.reshape(v.shape[0], 1, v.shape[1])
    conv_b_in3, conv_b_out3 = row3(conv_b_in), row3(conv_b_out)
    h = x.reshape(batch * seq, d)
    for i in range(depth):
        j = i // 2
        hn = _rmsnorm(h, norm_mix[i])
        if i % 2 == 0:
            gains = jnp.stack([attn_q_norm[j] * (HEAD_DIM ** -0.5 * LOG2E),
                               attn_k_norm[j]]).reshape(2, 1, HEAD_DIM)
            qkv = _dense_matmul(hn, [(attn_w_qkv, 0)], j, n_out=3 * d, tm=1024, tn=min(1024, d),
                                mode="headnorm", gains=gains, out_dtype=bf16)
            o = _attention(qkv, attn_rel_bias[j], batch, seq)
            h = _dense_matmul(o, [(attn_w_out, 0)], j, n_out=d, tm=1024, tn=512, res=h)
            hn = _rmsnorm(h, norm_ffn[i])
            act = _dense_matmul(hn, [(ffn_w_gate_up, 0), (ffn_w_gate_up, d_ff)], j, n_out=d_ff,
                                tm=1024, tn=512, mode="silu_mul", out_dtype=bf16)
            h = _dense_matmul(act, [(ffn_w_down, 0)], j, n_out=d, tm=512, tn=512, res=h)
        else:
            u = _dense_matmul(hn, [(conv_w_in, 0), (conv_w_in, d)], j, n_out=d, tm=1024,
                              tn=512, mode="mul_sig",
                              biases=[(conv_b_in3, 0), (conv_b_in3, d)])
            c = _conv_ln_swish(u, conv_w_dw, conv_b_dw, conv_ln_g, conv_ln_b, j, batch, seq)
            h = _dense_matmul(c, [(conv_w_out, 0)], j, n_out=d, tm=1024, tn=512,
                              biases=[(conv_b_out3, 0)], res=h)
            h = _moe(h, norm_ffn[i], moe_w_router[j], moe_w_gate, moe_w_up, moe_w_down, j)
    return h.reshape(batch, seq, d)
```

```python
import functools
import math

import jax
import jax.numpy as jnp
from jax import lax
from jax.experimental import pallas as pl
from jax.experimental.pallas import tpu as pltpu

CHUNK = 64
N_LEFT_CHUNKS = 8
HEAD_DIM = 128
MAX_REL = 128
CONV_WIDTH = 31
TOP_K = 2
RMS_EPS = 1e-6
LN_EPS = 1e-5

LANES = 128
SUBLANES = 8
V7X_VMEM_LIMIT = 56 * 1024 * 1024
MASK_VALUE = -1e30
LOG2E = math.log2(math.e)

ATTN_Q_ROWS = 4 * CHUNK
ATTN_PIECES = 3
ATTN_KEYS = ATTN_PIECES * ATTN_Q_ROWS
ATTN_TABLE = ATTN_KEYS + ATTN_Q_ROWS
CONV_ROWS = 64
CONV_HALO = 32
ISSUE_UNROLL = 8


def _params(sem, vmem=V7X_VMEM_LIMIT):
    return pltpu.CompilerParams(dimension_semantics=sem, vmem_limit_bytes=vmem)


def _rms_rows(x, g):
    ms = jnp.mean(x * x, axis=-1, keepdims=True)
    return x * lax.rsqrt(ms + RMS_EPS) * g


def _rmsnorm_kernel(x_ref, g_ref, o_ref):
    o_ref[...] = _rms_rows(x_ref[...], g_ref[...]).astype(o_ref.dtype)


def _rmsnorm(x, g, tm=256):
    m, d = x.shape
    tm = min(tm, m)
    return pl.pallas_call(
        _rmsnorm_kernel,
        grid=(m // tm,),
        in_specs=[pl.BlockSpec((tm, d), lambda i: (i, 0)),
                  pl.BlockSpec((1, d), lambda i: (0, 0))],
        out_specs=pl.BlockSpec((tm, d), lambda i: (i, 0)),
        out_shape=jax.ShapeDtypeStruct((m, d), jnp.bfloat16),
        compiler_params=_params(("parallel",)),
        name="rmsnorm",
    )(x, g.reshape(1, d))


def _epilogue(ys, j, o_ref, b_refs, res_ref, gain_ref, *, mode, heads_per_tile, n_norm_tiles):
    if b_refs:
        ys = [y + b_ref[...] for y, b_ref in zip(ys, b_refs)]
    if mode == "plain":
        y = ys[0]
        if res_ref is not None:
            y = y + res_ref[...]
        o_ref[...] = y.astype(o_ref.dtype)
    elif mode == "silu_mul":
        o_ref[...] = (ys[0] * jax.nn.sigmoid(ys[0]) * ys[1]).astype(o_ref.dtype)
    elif mode == "mul_sig":
        o_ref[...] = (ys[0] * jax.nn.sigmoid(ys[1])).astype(o_ref.dtype)
    else:
        y = ys[0]

        @pl.when(j < n_norm_tiles)
        def _():
            gain = gain_ref[...]
            for h in range(heads_per_tile):
                sl = slice(h * HEAD_DIM, (h + 1) * HEAD_DIM)
                o_ref[:, sl] = _rms_rows(y[:, sl], gain).astype(o_ref.dtype)

        @pl.when(j >= n_norm_tiles)
        def _():
            o_ref[...] = y.astype(o_ref.dtype)


def _split_refs(refs, n_w, has_bias, has_res, mode):
    refs = list(refs)
    a_ref = refs.pop(0)
    w_refs = [refs.pop(0) for _ in range(n_w)]
    b_refs = [refs.pop(0) for _ in range(n_w)] if has_bias else []
    res_ref = refs.pop(0) if has_res else None
    gain_ref = refs.pop(0) if mode == "headnorm" else None
    o_ref = refs.pop(0)
    return a_ref, w_refs, b_refs, res_ref, gain_ref, o_ref, refs


def _dense_kernel(*refs, n_w, mode, has_bias, has_res, nj, kc, **epi):
    a_ref, w_refs, b_refs, res_ref, gain_ref, o_ref, s_refs = _split_refs(
        refs, n_w, has_bias, has_res, mode)
    jj = pl.program_id(0)
    i = pl.program_id(1)
    rows = pl.ds(pl.multiple_of(i * kc, kc), kc)

    def convert(parity):
        for t, w_ref in enumerate(w_refs):
            s_refs[2 * t + parity][rows, :] = w_ref[...].astype(jnp.bfloat16)

    def multiply(parity):
        a = a_ref[...]
        ys = [jnp.dot(a, s_refs[2 * t + parity][...], preferred_element_type=jnp.float32)
              for t in range(n_w)]
        _epilogue(ys, jj - 1, o_ref, b_refs, res_ref, gain_ref, mode=mode, **epi)

    @pl.when(jj == 0)
    def _():
        convert(0)

    for parity in range(2):
        @pl.when(jnp.logical_and(jj > 0, jj % 2 == parity))
        def _():
            convert(parity)
            multiply(1 - parity)


def _dense_matmul(a, weights, layer, *, n_out, tm, tn, mode="plain", biases=None, res=None,
                  gains=None, out_dtype=jnp.float32):
    m, kdim = a.shape
    tm, tn = min(tm, m), min(tn, n_out)
    nj, ni = n_out // tn, m // tm
    kc = kdim // ni
    assert kc * ni == kdim and kc % 16 == 0, (kdim, ni)
    n_w = len(weights)
    col = lambda jj: jnp.maximum(jj - 1, 0)
    row = lambda jj, i: jnp.where(jj == 0, 0, i)

    in_specs = [pl.BlockSpec((tm, kdim), lambda jj, i: (row(jj, i), 0))]
    args = [a]
    for w, col0 in weights:
        off = col0 // tn
        in_specs.append(pl.BlockSpec(
            (None, kc, tn), lambda jj, i, off=off: (layer, i, jnp.minimum(jj, nj - 1) + off)))
        args.append(w)
    if biases is not None:
        for b, col0 in biases:
            off = col0 // tn
            in_specs.append(pl.BlockSpec((None, 1, tn),
                                         lambda jj, i, off=off: (layer, 0, col(jj) + off)))
            args.append(b)
    if res is not None:
        in_specs.append(pl.BlockSpec((tm, tn), lambda jj, i: (row(jj, i), col(jj))))
        args.append(res)
    heads_per_tile = n_norm_tiles = 0
    if mode == "headnorm":
        heads_per_tile = tn // HEAD_DIM
        tiles_per_section = (n_out // 3) // tn
        n_norm_tiles = 2 * tiles_per_section
        in_specs.append(pl.BlockSpec(
            (None, 1, HEAD_DIM),
            lambda jj, i: (jnp.minimum(col(jj) // tiles_per_section, 1), 0, 0)))
        args.append(gains)
    body = functools.partial(_dense_kernel, n_w=n_w, mode=mode, has_bias=biases is not None,
                             has_res=res is not None, nj=nj, kc=kc,
                             heads_per_tile=heads_per_tile, n_norm_tiles=n_norm_tiles)
    return pl.pallas_call(
        body, grid=(nj + 1, ni), in_specs=in_specs,
        out_specs=pl.BlockSpec((tm, tn), lambda jj, i: (row(jj, i), col(jj))),
        out_shape=jax.ShapeDtypeStruct((m, n_out), out_dtype),
        scratch_shapes=[pltpu.VMEM((kdim, tn), jnp.bfloat16) for _ in range(2 * n_w)],
        compiler_params=_params(("arbitrary", "arbitrary")),
        name="dense_matmul_" + mode,
    )(*args)


def _expert_kernel(te_ref, nu_ref, *refs, n_w, mode):
    a_ref, w_refs, _, _, _, o_ref, s_refs = _split_refs(refs, n_w, False, False, mode)
    j = pl.program_id(0)
    i = pl.program_id(1)
    fresh = jnp.logical_or(i == 0, te_ref[i] != te_ref[jnp.maximum(i - 1, 0)])

    @pl.when(fresh)
    def _():
        for w_ref, s_ref in zip(w_refs, s_refs):
            s_ref[...] = w_ref[...].astype(jnp.bfloat16)

    @pl.when(i < nu_ref[0])
    def _():
        a = a_ref[...]
        ys = [jnp.dot(a, s_ref[...], preferred_element_type=jnp.float32) for s_ref in s_refs]
        if mode == "packed":
            o_ref[...] = _pack_halves(ys[0])
        else:
            _epilogue(ys, j, o_ref, [], None, None, mode=mode, heads_per_tile=0,
                      n_norm_tiles=0)

    @pl.when(i >= nu_ref[0])
    def _():
        o_ref[...] = jnp.zeros(o_ref.shape, o_ref.dtype)


def _expert_matmul(a, weights, layer, tables, *, n_out, tm, tn, mode="plain",
                   out_dtype=jnp.float32):
    m, kdim = a.shape
    tn = min(tn, n_out)
    nj, ni = n_out // tn, m // tm
    w_spec = pl.BlockSpec((None, None, kdim, tn), lambda j, i, te, nu: (layer, te[i], 0, j))
    if mode == "packed":
        out_spec = pl.BlockSpec((tm, tn // 2), lambda j, i, te, nu: (i, j))
        out_shape = jax.ShapeDtypeStruct((m, n_out // 2), jnp.uint32)
    else:
        out_spec = pl.BlockSpec((tm, tn), lambda j, i, te, nu: (i, j))
        out_shape = jax.ShapeDtypeStruct((m, n_out), out_dtype)
    return pl.pallas_call(
        functools.partial(_expert_kernel, n_w=len(weights), mode=mode),
        grid_spec=pltpu.PrefetchScalarGridSpec(
            num_scalar_prefetch=2, grid=(nj, ni),
            in_specs=[pl.BlockSpec((tm, kdim), lambda j, i, te, nu: (i, 0))]
            + [w_spec] * len(weights),
            out_specs=out_spec,
            scratch_shapes=[pltpu.VMEM((kdim, tn), jnp.bfloat16) for _ in weights]),
        out_shape=out_shape,
        compiler_params=_params(("arbitrary", "arbitrary")),
        name="expert_matmul_" + mode,
    )(*tables, a, *weights)


def _attn_kernel(q_ref, k0_ref, k1_ref, k2_ref, v0_ref, v1_ref, v2_ref, tab_ref, o_ref,
                 bias_ref, *, heads):
    b = pl.program_id(1)
    i = pl.program_id(2)
    rows = ATTN_Q_ROWS

    @pl.when(jnp.logical_and(b == 0, i == 0))
    def _():
        qi = lax.broadcasted_iota(jnp.int32, (rows, ATTN_KEYS), 0) // CHUNK
        kc = lax.broadcasted_iota(jnp.int32, (rows, ATTN_KEYS), 1) // CHUNK
        in_band = jnp.logical_and(kc >= qi, kc <= qi + N_LEFT_CHUNKS)
        for h in range(heads):
            tab = jnp.broadcast_to(tab_ref[h], (rows, ATTN_TABLE))
            rolled = pltpu.roll(tab, 0, 1, stride=1, stride_axis=0)
            bias_ref[h] = jnp.where(in_band, rolled[:, :ATTN_KEYS], MASK_VALUE)

    k_refs = (k0_ref, k1_ref, k2_ref)
    v_refs = (v0_ref, v1_ref, v2_ref)

    def scores(h):
        sl = slice(h * HEAD_DIM, (h + 1) * HEAD_DIM)
        q = q_ref[:, sl]
        s = []
        for p in range(ATTN_PIECES):
            sp = lax.dot_general(q, k_refs[p][:, sl], (((1,), (1,)), ((), ())),
                                 preferred_element_type=jnp.float32)
            sp = sp + bias_ref[h, :, p * rows:(p + 1) * rows]
            if p < ATTN_PIECES - 1:
                sp = jnp.where(i + p >= ATTN_PIECES - 1, sp, MASK_VALUE)
            s.append(sp)
        return s

    s_next = scores(0)
    for h in range(heads):
        sl = slice(h * HEAD_DIM, (h + 1) * HEAD_DIM)
        s = s_next
        if h + 1 < heads:
            s_next = scores(h + 1)
        mx = jnp.max(s[0], axis=-1, keepdims=True)
        for p in range(1, ATTN_PIECES):
            mx = jnp.maximum(mx, jnp.max(s[p], axis=-1, keepdims=True))
        denom = None
        out = None
        for p in range(ATTN_PIECES):
            e = jnp.exp2(s[p] - mx)
            dsum = jnp.sum(e, axis=-1, keepdims=True)
            pv = jnp.dot(e.astype(jnp.bfloat16), v_refs[p][:, sl],
                         preferred_element_type=jnp.float32)
            denom = dsum if denom is None else denom + dsum
            out = pv if out is None else out + pv
        o_ref[:, sl] = (out / denom).astype(o_ref.dtype)


def _bias_rows(rel_bias):
    c = jnp.arange(ATTN_TABLE)
    dist = (ATTN_PIECES - 1) * ATTN_Q_ROWS - c
    dist = jnp.where(c < ATTN_KEYS, dist, MAX_REL)
    rel = jnp.clip(dist, -MAX_REL, MAX_REL) + MAX_REL
    rows = rel_bias.astype(jnp.float32)[:, rel] * LOG2E
    return rows.reshape(rel_bias.shape[0], 1, ATTN_TABLE)


def _attention(qkv, rel_bias, batch, seq, heads_per_step=8):
    t, d3 = qkv.shape
    d = d3 // 3
    n_heads = d // HEAD_DIM
    hg = min(heads_per_step, n_heads)
    n_groups = n_heads // hg
    rows = ATTN_Q_ROWS
    nblk = seq // rows
    width = hg * HEAD_DIM

    def q_map(g, b, i):
        return (b * nblk + i, g)

    def kv_map(section, back):
        def f(g, b, i):
            return (b * nblk + jnp.maximum(i - back, 0), section * n_groups + g)
        return f

    blk = lambda f: pl.BlockSpec((rows, width), f)
    return pl.pallas_call(
        functools.partial(_attn_kernel, heads=hg),
        grid=(n_groups, batch, nblk),
        in_specs=[blk(q_map),
                  blk(kv_map(1, 2)), blk(kv_map(1, 1)), blk(kv_map(1, 0)),
                  blk(kv_map(2, 2)), blk(kv_map(2, 1)), blk(kv_map(2, 0)),
                  pl.BlockSpec((hg, 1, ATTN_TABLE), lambda g, b, i: (g, 0, 0))],
        out_specs=blk(q_map),
        out_shape=jax.ShapeDtypeStruct((t, d), jnp.bfloat16),
        scratch_shapes=[pltpu.VMEM((hg, rows, ATTN_KEYS), jnp.float32)],
        compiler_params=_params(("arbitrary", "arbitrary", "arbitrary")),
        name="band_attention",
    )(qkv, qkv, qkv, qkv, qkv, qkv, qkv, _bias_rows(rel_bias))


def _conv_kernel(prev_ref, cur_ref, w_ref, bdw_ref, g_ref, b_ref, o_ref, buf_ref, cv_ref,
                 *, col_chunk):
    i = pl.program_id(1)
    d = cur_ref.shape[1]
    halo = CONV_HALO
    buf_ref[0:halo, :] = jnp.where(i > 0, prev_ref[...], 0.0)
    buf_ref[halo:halo + CONV_ROWS, :] = cur_ref[...]
    first = halo - (CONV_WIDTH - 1)
    for c0 in range(0, d, col_chunk):
        cs = slice(c0, c0 + col_chunk)
        acc = jnp.zeros((CONV_ROWS, col_chunk), jnp.float32) + bdw_ref[:, cs]
        for s in range(SUBLANES):
            taps = [t for t in range(CONV_WIDTH) if (first + t) % SUBLANES == s]
            n_rows = CONV_ROWS + (SUBLANES if s else 0)
            part = None
            for t in taps:
                start = first + t - s
                wt = w_ref[t * SUBLANES:(t + 1) * SUBLANES, cs][None]
                term = buf_ref[start:start + n_rows, cs].reshape(-1, SUBLANES, col_chunk) * wt
                part = term if part is None else part + term
            part = part.reshape(n_rows, col_chunk)
            if s:
                part = pltpu.roll(part, n_rows - s, 0)[:CONV_ROWS, :]
            acc = acc + part
        cv_ref[:, cs] = acc
    x = cv_ref[...]
    mu = jnp.mean(x, axis=-1, keepdims=True)
    xc = x - mu
    var = jnp.mean(xc * xc, axis=-1, keepdims=True)
    y = xc * lax.rsqrt(var + LN_EPS) * g_ref[...] + b_ref[...]
    o_ref[...] = (y * jax.nn.sigmoid(y)).astype(o_ref.dtype)


def _conv_ln_swish(u, w_dw, b_dw, ln_g, ln_b, layer, batch, seq):
    t, d = u.shape
    nblk = seq // CONV_ROWS
    ratio = CONV_ROWS // CONV_HALO
    row3 = lambda v: v.reshape(v.shape[0], 1, d)
    vec = pl.BlockSpec((None, 1, d), lambda b, i: (layer, 0, 0))
    return pl.pallas_call(
        functools.partial(_conv_kernel, col_chunk=min(256, d)),
        grid=(batch, nblk),
        in_specs=[pl.BlockSpec((CONV_HALO, d),
                               lambda b, i: (jnp.maximum((b * nblk + i) * ratio - 1, 0), 0)),
                  pl.BlockSpec((CONV_ROWS, d), lambda b, i: (b * nblk + i, 0)),
                  pl.BlockSpec((None, CONV_WIDTH * SUBLANES, d), lambda b, i: (layer, 0, 0)),
                  vec, vec, vec],
        out_specs=pl.BlockSpec((CONV_ROWS, d), lambda b, i: (b * nblk + i, 0)),
        out_shape=jax.ShapeDtypeStruct((t, d), jnp.bfloat16),
        scratch_shapes=[pltpu.VMEM((CONV_HALO + CONV_ROWS, d), jnp.float32),
                        pltpu.VMEM((CONV_ROWS, d), jnp.float32)],
        compiler_params=_params(("parallel", "parallel")),
        name="conv_ln_swish",
    )(u, u, jnp.repeat(w_dw, SUBLANES, axis=1), row3(b_dw), row3(ln_g), row3(ln_b))


def _pack_halves(x):
    half = x.shape[1] // 2
    bits = pltpu.bitcast(x.astype(jnp.bfloat16).astype(jnp.float32), jnp.uint32)
    return (bits[:, :half] >> 16) | bits[:, half:]


def _unpack_halves(words):
    lo = pltpu.bitcast(words << 16, jnp.float32)
    hi = pltpu.bitcast(words & jnp.uint32(0xFFFF0000), jnp.float32)
    return lo, hi


def _router_kernel(x_ref, g_ref, whi_ref, wlo_ref, idx_ref, wt_ref, xp_ref, *, n_experts):
    hn = _rms_rows(x_ref[...], g_ref[...])
    xp_ref[...] = _pack_halves(hn)
    hi = hn.astype(jnp.bfloat16)
    lo = (hn - hi.astype(jnp.float32)).astype(jnp.bfloat16)
    dot = functools.partial(jnp.dot, preferred_element_type=jnp.float32)
    logits = dot(hi, whi_ref[...]) + (dot(hi, wlo_ref[...]) + dot(lo, whi_ref[...]))
    lane = lax.broadcasted_iota(jnp.int32, logits.shape, 1)
    logits = jnp.where(lane < n_experts, logits, -jnp.inf)
    v1 = jnp.max(logits, axis=-1, keepdims=True)
    i1 = jnp.min(jnp.where(logits == v1, lane, LANES), axis=-1, keepdims=True)
    rest = jnp.where(lane == i1, -jnp.inf, logits)
    v2 = jnp.max(rest, axis=-1, keepdims=True)
    i2 = jnp.min(jnp.where(rest == v2, lane, LANES), axis=-1, keepdims=True)
    e2 = jnp.exp(v2 - v1)
    denom = 1.0 + e2
    idx_ref[...] = jnp.where(lane == 0, i1, i2)
    wt_ref[...] = jnp.where(lane == 0, 1.0 / denom, e2 / denom)


def _router(h, g, w_router, tm=256):
    t, d = h.shape
    n_experts = w_router.shape[1]
    tm = min(tm, t)
    w_pad = jnp.zeros((d, LANES), jnp.float32).at[:, :n_experts].set(w_router)
    w_hi = w_pad.astype(jnp.bfloat16)
    w_lo = (w_pad - w_hi.astype(jnp.float32)).astype(jnp.bfloat16)
    return pl.pallas_call(
        functools.partial(_router_kernel, n_experts=n_experts),
        grid=(t // tm,),
        in_specs=[pl.BlockSpec((tm, d), lambda i: (i, 0)),
                  pl.BlockSpec((1, d), lambda i: (0, 0)),
                  pl.BlockSpec((d, LANES), lambda i: (0, 0)),
                  pl.BlockSpec((d, LANES), lambda i: (0, 0))],
        out_specs=[pl.BlockSpec((tm, LANES), lambda i: (i, 0)),
                   pl.BlockSpec((tm, LANES), lambda i: (i, 0)),
                   pl.BlockSpec((tm, d // 2), lambda i: (i, 0))],
        out_shape=[jax.ShapeDtypeStruct((t, LANES), jnp.int32),
                   jax.ShapeDtypeStruct((t, LANES), jnp.float32),
                   jax.ShapeDtypeStruct((t, d // 2), jnp.uint32)],
        compiler_params=_params(("parallel",)),
        name="router",
    )(h, g.reshape(1, d), w_hi, w_lo)


def _row_copy(src_hbm, row, dst, slot, sem):
    return pltpu.make_async_copy(src_hbm.at[pl.ds(row, 1), :], dst.at[pl.ds(slot, 1), :], sem)


def _dispatch_kernel(src_ref, nu_ref, xp_hbm, o_ref, buf_ref, sem, *, rows, tiles_per_mm):
    step = pl.program_id(0)
    n_used = nu_ref[0] * tiles_per_mm
    half = buf_ref.shape[2]

    def gather(tile):
        slot = tile % 2
        base = tile * rows

        def issue(r, carry):
            _row_copy(xp_hbm, src_ref[base + r], buf_ref.at[slot], r, sem.at[slot]).start()
            return carry

        lax.fori_loop(0, rows, issue, 0, unroll=ISSUE_UNROLL)

    @pl.when(jnp.logical_and(step == 0, n_used > 0))
    def _():
        gather(step)

    @pl.when(step + 1 < n_used)
    def _():
        gather(step + 1)

    @pl.when(step < n_used)
    def _():
        slot = step % 2
        pltpu.make_async_copy(xp_hbm.at[pl.ds(0, rows), :], buf_ref.at[slot], sem.at[slot]).wait()
        lo, hi = _unpack_halves(buf_ref[slot])
        o_ref[:, :half] = lo.astype(o_ref.dtype)
        o_ref[:, half:] = hi.astype(o_ref.dtype)

    @pl.when(step >= n_used)
    def _():
        o_ref[...] = jnp.zeros(o_ref.shape, o_ref.dtype)


def _dispatch(xp, src, n_used, tm, rows=256):
    t, half = xp.shape
    p = src.shape[0]
    rows = min(rows, tm)
    return pl.pallas_call(
        functools.partial(_dispatch_kernel, rows=rows, tiles_per_mm=tm // rows),
        grid_spec=pltpu.PrefetchScalarGridSpec(
            num_scalar_prefetch=2,
            grid=(p // rows,),
            in_specs=[pl.BlockSpec(memory_space=pl.ANY)],
            out_specs=pl.BlockSpec((rows, 2 * half), lambda i, s, nu: (i, 0)),
            scratch_shapes=[pltpu.VMEM((2, rows, half), jnp.uint32),
                            pltpu.SemaphoreType.DMA((2,))]),
        out_shape=jax.ShapeDtypeStruct((p, 2 * half), jnp.bfloat16),
        compiler_params=_params(("arbitrary",)),
        name="moe_dispatch",
    )(src, n_used, xp)


def _combine_kernel(pos_ref, ys_hbm, h_ref, wt_ref, o_ref, buf_ref, sem, *, rows, tn):
    step = pl.program_id(0)
    n_steps = pl.num_programs(0)

    def gather(tile):
        slot = tile % 2
        base = tile * rows

        def issue(r, carry):
            for k in range(TOP_K):
                _row_copy(ys_hbm, pos_ref[TOP_K * (base + r) + k], buf_ref.at[slot, k], r,
                          sem.at[slot]).start()
            return carry

        lax.fori_loop(0, rows, issue, 0, unroll=ISSUE_UNROLL)

    @pl.when(step == 0)
    def _():
        gather(step)

    @pl.when(step + 1 < n_steps)
    def _():
        gather(step + 1)

    slot = step % 2
    for k in range(TOP_K):
        pltpu.make_async_copy(ys_hbm.at[pl.ds(0, rows), :], buf_ref.at[slot, k],
                              sem.at[slot]).wait()
    wt = wt_ref[...]
    half = tn // 2
    for jt in range(o_ref.shape[1] // tn):
        lo_cols = slice(jt * tn, jt * tn + half)
        hi_cols = slice(jt * tn + half, (jt + 1) * tn)
        acc_lo = h_ref[:, lo_cols]
        acc_hi = h_ref[:, hi_cols]
        for k in range(TOP_K):
            lo, hi = _unpack_halves(buf_ref[slot, k, :, jt * half:(jt + 1) * half])
            acc_lo = acc_lo + wt[:, k:k + 1] * lo
            acc_hi = acc_hi + wt[:, k:k + 1] * hi
        o_ref[:, lo_cols] = acc_lo
        o_ref[:, hi_cols] = acc_hi


def _combine(h, ys, pos, wts, tn, rows=128):
    t, d = h.shape
    rows = min(rows, t)
    return pl.pallas_call(
        functools.partial(_combine_kernel, rows=rows, tn=tn),
        grid_spec=pltpu.PrefetchScalarGridSpec(
            num_scalar_prefetch=1,
            grid=(t // rows,),
            in_specs=[pl.BlockSpec(memory_space=pl.ANY),
                      pl.BlockSpec((rows, d), lambda i, s: (i, 0)),
                      pl.BlockSpec((rows, LANES), lambda i, s: (i, 0))],
            out_specs=pl.BlockSpec((rows, d), lambda i, s: (i, 0)),
            scratch_shapes=[pltpu.VMEM((2, TOP_K, rows, d // 2), jnp.uint32),
                            pltpu.SemaphoreType.DMA((2,))]),
        out_shape=jax.ShapeDtypeStruct((t, d), jnp.float32),
        compiler_params=_params(("arbitrary",)),
        name="moe_combine",
    )(pos, ys, h, wts)


def _route_tables(idx, n_experts, tm):
    t = idx.shape[0]
    n_assign = t * TOP_K
    p = n_assign + n_experts * tm
    flat = idx.reshape(-1)
    onehot = (flat[:, None] == jnp.arange(n_experts)[None, :]).astype(jnp.int32)
    csum = jnp.cumsum(onehot, axis=0)
    counts = csum[-1]
    rank = jnp.sum(csum * onehot, axis=1) - 1
    padded = ((counts + tm - 1) // tm) * tm
    ends = jnp.cumsum(padded)
    starts = ends - padded
    pos = jnp.sum(starts[None, :] * onehot, axis=1) + rank
    src = jnp.zeros((p,), jnp.int32).at[pos].set(jnp.arange(n_assign, dtype=jnp.int32) // TOP_K)
    tile_start = jnp.arange(p // tm, dtype=jnp.int32) * tm
    tile_expert = jnp.sum((tile_start[:, None] >= ends[None, :]).astype(jnp.int32), axis=1)
    tile_expert = jnp.minimum(tile_expert, n_experts - 1)
    n_used = (ends[-1:] // tm).astype(jnp.int32)
    return pos.astype(jnp.int32), src, tile_expert.astype(jnp.int32), n_used


def _moe(h, g, w_router, w_gate, w_up, w_down, layer, tm=512):
    n_experts = w_router.shape[1]
    f = w_gate.shape[3]
    d = h.shape[1]
    idx_pad, wts, xp = _router(h, g, w_router)
    pos, src, tile_expert, n_used = _route_tables(idx_pad[:, :TOP_K], n_experts, tm)
    xs = _dispatch(xp, src, n_used, tm)
    tables = (tile_expert, n_used)
    act = _expert_matmul(xs, [w_gate, w_up], layer, tables, n_out=f, tm=tm, tn=512,
                         mode="silu_mul", out_dtype=jnp.bfloat16)
    tn_down = min(1024, d)
    ys = _expert_matmul(act, [w_down], layer, tables, n_out=d, tm=tm, tn=tn_down,
                        mode="packed")
    return _combine(h, ys, pos, wts, tn_down)


def kernel(x, norm_mix, norm_ffn, attn_w_qkv, attn_q_norm, attn_k_norm, attn_rel_bias,
           attn_w_out, conv_w_in, conv_b_in, conv_w_dw, conv_b_dw, conv_ln_g, conv_ln_b,
           conv_w_out, conv_b_out, ffn_w_gate_up, ffn_w_down, moe_w_router, moe_w_gate,
           moe_w_up, moe_w_down):
    batch, seq, d = x.shape
    depth = norm_mix.shape[0]
    d_ff = ffn_w_gate_up.shape[2] // 2
    bf16 = jnp.bfloat16
    row3 = lambda v: v.reshape(v.shape[0], 1, v.shape[1])
    conv_b_in3, conv_b_out3 = row3(conv_b_in), row3(conv_b_out)
    h = x.reshape(batch * seq, d)
    for i in range(depth):
        j = i // 2
        hn = _rmsnorm(h, norm_mix[i])
        if i % 2 == 0:
            gains = jnp.stack([attn_q_norm[j] * (HEAD_DIM ** -0.5 * LOG2E),
                               attn_k_norm[j]]).reshape(2, 1, HEAD_DIM)
            qkv = _dense_matmul(hn, [(attn_w_qkv, 0)], j, n_out=3 * d, tm=1024, tn=min(1024, d),
                                mode="headnorm", gains=gains, out_dtype=bf16)
            o = _attention(qkv, attn_rel_bias[j], batch, seq)
            h = _dense_matmul(o, [(attn_w_out, 0)], j, n_out=d, tm=1024, tn=512, res=h)
            hn = _rmsnorm(h, norm_ffn[i])
            act = _dense_matmul(hn, [(ffn_w_gate_up, 0), (ffn_w_gate_up, d_ff)], j, n_out=d_ff,
                                tm=1024, tn=512, mode="silu_mul", out_dtype=bf16)
            h = _dense_matmul(act, [(ffn_w_down, 0)], j, n_out=d, tm=512, tn=512, res=h)
        else:
            u = _dense_matmul(hn, [(conv_w_in, 0), (conv_w_in, d)], j, n_out=d, tm=1024,
                              tn=512, mode="mul_sig",
                              biases=[(conv_b_in3, 0), (conv_b_in3, d)])
            c = _conv_ln_swish(u, conv_w_dw, conv_b_dw, conv_ln_g, conv_ln_b, j, batch, seq)
            h = _dense_matmul(c, [(conv_w_out, 0)], j, n_out=d, tm=1024, tn=512,
                              biases=[(conv_b_out3, 0)], res=h)
            h = _moe(h, norm_ffn[i], moe_w_router[j], moe_w_gate, moe_w_up, moe_w_down, j)
    return h.reshape(batch, seq, d)
```

```python
import functools
import math

import jax
import jax.numpy as jnp
from jax import lax
from jax.experimental import pallas as pl
from jax.experimental.pallas import tpu as pltpu

CHUNK = 64
N_LEFT_CHUNKS = 8
HEAD_DIM = 128
MAX_REL = 128
CONV_WIDTH = 31
TOP_K = 2
RMS_EPS = 1e-6
LN_EPS = 1e-5

LANES = 128
SUBLANES = 8
V7X_VMEM_LIMIT = 56 * 1024 * 1024
MASK_VALUE = -1e30
LOG2E = math.log2(math.e)

ATTN_Q_ROWS = 4 * CHUNK
ATTN_PIECES = 3
ATTN_KEYS = ATTN_PIECES * ATTN_Q_ROWS
ATTN_TABLE = ATTN_KEYS + ATTN_Q_ROWS
CONV_ROWS = 128
CONV_HALO = 32
ISSUE_UNROLL = 8


def _params(sem, vmem=V7X_VMEM_LIMIT):
    return pltpu.CompilerParams(dimension_semantics=sem, vmem_limit_bytes=vmem)


def _rms_rows(x, g):
    ms = jnp.mean(x * x, axis=-1, keepdims=True)
    return x * lax.rsqrt(ms + RMS_EPS) * g


def _rmsnorm_kernel(x_ref, g_ref, o_ref):
    o_ref[...] = _rms_rows(x_ref[...], g_ref[...]).astype(o_ref.dtype)


def _rmsnorm(x, g, tm=256):
    m, d = x.shape
    tm = min(tm, m)
    return pl.pallas_call(
        _rmsnorm_kernel,
        grid=(m // tm,),
        in_specs=[pl.BlockSpec((tm, d), lambda i: (i, 0)),
                  pl.BlockSpec((1, d), lambda i: (0, 0))],
        out_specs=pl.BlockSpec((tm, d), lambda i: (i, 0)),
        out_shape=jax.ShapeDtypeStruct((m, d), jnp.bfloat16),
        compiler_params=_params(("parallel",)),
        name="rmsnorm",
    )(x, g.reshape(1, d))


def _epilogue(ys, j, o_ref, b_refs, res_ref, gain_ref, *, mode, heads_per_tile, n_norm_tiles):
    if b_refs:
        ys = [y + b_ref[...] for y, b_ref in zip(ys, b_refs)]
    if mode == "plain":
        y = ys[0]
        if res_ref is not None:
            y = y + res_ref[...]
        o_ref[...] = y.astype(o_ref.dtype)
    elif mode == "silu_mul":
        o_ref[...] = (ys[0] * jax.nn.sigmoid(ys[0]) * ys[1]).astype(o_ref.dtype)
    elif mode == "mul_sig":
        o_ref[...] = (ys[0] * jax.nn.sigmoid(ys[1])).astype(o_ref.dtype)
    else:
        y = ys[0]

        @pl.when(j < n_norm_tiles)
        def _():
            gain = gain_ref[...]
            for h in range(heads_per_tile):
                sl = slice(h * HEAD_DIM, (h + 1) * HEAD_DIM)
                o_ref[:, sl] = _rms_rows(y[:, sl], gain).astype(o_ref.dtype)

        @pl.when(j >= n_norm_tiles)
        def _():
            o_ref[...] = y.astype(o_ref.dtype)


def _split_refs(refs, n_w, has_bias, has_res, mode):
    refs = list(refs)
    a_ref = refs.pop(0)
    w_refs = [refs.pop(0) for _ in range(n_w)]
    b_refs = [refs.pop(0) for _ in range(n_w)] if has_bias else []
    res_ref = refs.pop(0) if has_res else None
    gain_ref = refs.pop(0) if mode == "headnorm" else None
    o_ref = refs.pop(0)
    return a_ref, w_refs, b_refs, res_ref, gain_ref, o_ref, refs


def _dense_kernel(*refs, n_w, mode, has_bias, has_res, nj, kc, **epi):
    a_ref, w_refs, b_refs, res_ref, gain_ref, o_ref, s_refs = _split_refs(
        refs, n_w, has_bias, has_res, mode)
    jj = pl.program_id(0)
    i = pl.program_id(1)
    rows = pl.ds(pl.multiple_of(i * kc, kc), kc)

    def convert(parity):
        for t, w_ref in enumerate(w_refs):
            s_refs[2 * t + parity][rows, :] = w_ref[...].astype(jnp.bfloat16)

    def multiply(parity):
        a = a_ref[...]
        ys = [jnp.dot(a, s_refs[2 * t + parity][...], preferred_element_type=jnp.float32)
              for t in range(n_w)]
        _epilogue(ys, jj - 1, o_ref, b_refs, res_ref, gain_ref, mode=mode, **epi)

    @pl.when(jj == 0)
    def _():
        convert(0)

    for parity in range(2):
        @pl.when(jnp.logical_and(jj > 0, jj % 2 == parity))
        def _():
            convert(parity)
            multiply(1 - parity)


def _dense_matmul(a, weights, layer, *, n_out, tm, tn, mode="plain", biases=None, res=None,
                  gains=None, out_dtype=jnp.float32):
    m, kdim = a.shape
    tm, tn = min(tm, m), min(tn, n_out)
    nj, ni = n_out // tn, m // tm
    kc = kdim // ni
    assert kc * ni == kdim and kc % 16 == 0, (kdim, ni)
    n_w = len(weights)
    col = lambda jj: jnp.maximum(jj - 1, 0)
    row = lambda jj, i: jnp.where(jj == 0, 0, i)

    in_specs = [pl.BlockSpec((tm, kdim), lambda jj, i: (row(jj, i), 0))]
    args = [a]
    for w, col0 in weights:
        off = col0 // tn
        in_specs.append(pl.BlockSpec(
            (None, kc, tn), lambda jj, i, off=off: (layer, i, jnp.minimum(jj, nj - 1) + off)))
        args.append(w)
    if biases is not None:
        for b, col0 in biases:
            off = col0 // tn
            in_specs.append(pl.BlockSpec((None, 1, tn),
                                         lambda jj, i, off=off: (layer, 0, col(jj) + off)))
            args.append(b)
    if res is not None:
        in_specs.append(pl.BlockSpec((tm, tn), lambda jj, i: (row(jj, i), col(jj))))
        args.append(res)
    heads_per_tile = n_norm_tiles = 0
    if mode == "headnorm":
        heads_per_tile = tn // HEAD_DIM
        tiles_per_section = (n_out // 3) // tn
        n_norm_tiles = 2 * tiles_per_section
        in_specs.append(pl.BlockSpec(
            (None, 1, HEAD_DIM),
            lambda jj, i: (jnp.minimum(col(jj) // tiles_per_section, 1), 0, 0)))
        args.append(gains)
    body = functools.partial(_dense_kernel, n_w=n_w, mode=mode, has_bias=biases is not None,
                             has_res=res is not None, nj=nj, kc=kc,
                             heads_per_tile=heads_per_tile, n_norm_tiles=n_norm_tiles)
    return pl.pallas_call(
        body, grid=(nj + 1, ni), in_specs=in_specs,
        out_specs=pl.BlockSpec((tm, tn), lambda jj, i: (row(jj, i), col(jj))),
        out_shape=jax.ShapeDtypeStruct((m, n_out), out_dtype),
        scratch_shapes=[pltpu.VMEM((kdim, tn), jnp.bfloat16) for _ in range(2 * n_w)],
        compiler_params=_params(("arbitrary", "arbitrary")),
        name="dense_matmul_" + mode,
    )(*args)


def _expert_kernel(te_ref, nu_ref, *refs, n_w, mode):
    a_ref, w_refs, _, _, _, o_ref, s_refs = _split_refs(refs, n_w, False, False, mode)
    j = pl.program_id(0)
    i = pl.program_id(1)
    fresh = jnp.logical_or(i == 0, te_ref[i] != te_ref[jnp.maximum(i - 1, 0)])

    @pl.when(fresh)
    def _():
        for w_ref, s_ref in zip(w_refs, s_refs):
            s_ref[...] = w_ref[...].astype(jnp.bfloat16)

    @pl.when(i < nu_ref[0])
    def _():
        a = a_ref[...]
        ys = [jnp.dot(a, s_ref[...], preferred_element_type=jnp.float32) for s_ref in s_refs]
        if mode == "packed":
            o_ref[...] = _pack_halves(ys[0])
        else:
            _epilogue(ys, j, o_ref, [], None, None, mode=mode, heads_per_tile=0,
                      n_norm_tiles=0)

    @pl.when(i >= nu_ref[0])
    def _():
        o_ref[...] = jnp.zeros(o_ref.shape, o_ref.dtype)


def _expert_matmul(a, weights, layer, tables, *, n_out, tm, tn, mode="plain",
                   out_dtype=jnp.float32):
    m, kdim = a.shape
    tn = min(tn, n_out)
    nj, ni = n_out // tn, m // tm
    w_spec = pl.BlockSpec((None, None, kdim, tn), lambda j, i, te, nu: (layer, te[i], 0, j))
    if mode == "packed":
        out_spec = pl.BlockSpec((tm, tn // 2), lambda j, i, te, nu: (i, j))
        out_shape = jax.ShapeDtypeStruct((m, n_out // 2), jnp.uint32)
    else:
        out_spec = pl.BlockSpec((tm, tn), lambda j, i, te, nu: (i, j))
        out_shape = jax.ShapeDtypeStruct((m, n_out), out_dtype)
    return pl.pallas_call(
        functools.partial(_expert_kernel, n_w=len(weights), mode=mode),
        grid_spec=pltpu.PrefetchScalarGridSpec(
            num_scalar_prefetch=2, grid=(nj, ni),
            in_specs=[pl.BlockSpec((tm, kdim), lambda j, i, te, nu: (i, 0))]
            + [w_spec] * len(weights),
            out_specs=out_spec,
            scratch_shapes=[pltpu.VMEM((kdim, tn), jnp.bfloat16) for _ in weights]),
        out_shape=out_shape,
        compiler_params=_params(("arbitrary", "arbitrary")),
        name="expert_matmul_" + mode,
    )(*tables, a, *weights)


def _attn_kernel(q_ref, k0_ref, k1_ref, k2_ref, v0_ref, v1_ref, v2_ref, tab_ref, o_ref,
                 bias_ref, *, heads):
    b = pl.program_id(1)
    i = pl.program_id(2)
    rows = ATTN_Q_ROWS

    @pl.when(jnp.logical_and(b == 0, i == 0))
    def _():
        qi = lax.broadcasted_iota(jnp.int32, (rows, ATTN_KEYS), 0) // CHUNK
        kc = lax.broadcasted_iota(jnp.int32, (rows, ATTN_KEYS), 1) // CHUNK
        in_band = jnp.logical_and(kc >= qi, kc <= qi + N_LEFT_CHUNKS)
        for h in range(heads):
            tab = jnp.broadcast_to(tab_ref[h], (rows, ATTN_TABLE))
            rolled = pltpu.roll(tab, 0, 1, stride=1, stride_axis=0)
            bias_ref[h] = jnp.where(in_band, rolled[:, :ATTN_KEYS], MASK_VALUE)

    k_refs = (k0_ref, k1_ref, k2_ref)
    v_refs = (v0_ref, v1_ref, v2_ref)

    def scores(h):
        sl = slice(h * HEAD_DIM, (h + 1) * HEAD_DIM)
        q = q_ref[:, sl]
        s = []
        for p in range(ATTN_PIECES):
            sp = lax.dot_general(q, k_refs[p][:, sl], (((1,), (1,)), ((), ())),
                                 preferred_element_type=jnp.float32)
            sp = sp + bias_ref[h, :, p * rows:(p + 1) * rows]
            if p < ATTN_PIECES - 1:
                sp = jnp.where(i + p >= ATTN_PIECES - 1, sp, MASK_VALUE)
            s.append(sp)
        return s

    s_next = scores(0)
    for h in range(heads):
        sl = slice(h * HEAD_DIM, (h + 1) * HEAD_DIM)
        s = s_next
        if h + 1 < heads:
            s_next = scores(h + 1)
        mx = jnp.max(s[0], axis=-1, keepdims=True)
        for p in range(1, ATTN_PIECES):
            mx = jnp.maximum(mx, jnp.max(s[p], axis=-1, keepdims=True))
        denom = None
        out = None
        for p in range(ATTN_PIECES):
            e = jnp.exp2(s[p] - mx)
            dsum = jnp.sum(e, axis=-1, keepdims=True)
            pv = jnp.dot(e.astype(jnp.bfloat16), v_refs[p][:, sl],
                         preferred_element_type=jnp.float32)
            denom = dsum if denom is None else denom + dsum
            out = pv if out is None else out + pv
        o_ref[:, sl] = (out / denom).astype(o_ref.dtype)


def _bias_rows(rel_bias):
    c = jnp.arange(ATTN_TABLE)
    dist = (ATTN_PIECES - 1) * ATTN_Q_ROWS - c
    dist = jnp.where(c < ATTN_KEYS, dist, MAX_REL)
    rel = jnp.clip(dist, -MAX_REL, MAX_REL) + MAX_REL
    rows = rel_bias.astype(jnp.float32)[:, rel] * LOG2E
    return rows.reshape(rel_bias.shape[0], 1, ATTN_TABLE)


def _attention(qkv, rel_bias, batch, seq, heads_per_step=8):
    t, d3 = qkv.shape
    d = d3 // 3
    n_heads = d // HEAD_DIM
    hg = min(heads_per_step, n_heads)
    n_groups = n_heads // hg
    rows = ATTN_Q_ROWS
    nblk = seq // rows
    width = hg * HEAD_DIM

    def q_map(g, b, i):
        return (b * nblk + i, g)

    def kv_map(section, back):
        def f(g, b, i):
            return (b * nblk + jnp.maximum(i - back, 0), section * n_groups + g)
        return f

    blk = lambda f: pl.BlockSpec((rows, width), f)
    return pl.pallas_call(
        functools.partial(_attn_kernel, heads=hg),
        grid=(n_groups, batch, nblk),
        in_specs=[blk(q_map),
                  blk(kv_map(1, 2)), blk(kv_map(1, 1)), blk(kv_map(1, 0)),
                  blk(kv_map(2, 2)), blk(kv_map(2, 1)), blk(kv_map(2, 0)),
                  pl.BlockSpec((hg, 1, ATTN_TABLE), lambda g, b, i: (g, 0, 0))],
        out_specs=blk(q_map),
        out_shape=jax.ShapeDtypeStruct((t, d), jnp.bfloat16),
        scratch_shapes=[pltpu.VMEM((hg, rows, ATTN_KEYS), jnp.float32)],
        compiler_params=_params(("arbitrary", "arbitrary", "arbitrary")),
        name="band_attention",
    )(qkv, qkv, qkv, qkv, qkv, qkv, qkv, _bias_rows(rel_bias))


def _conv_kernel(prev_ref, cur_ref, w_ref, bdw_ref, g_ref, b_ref, o_ref, buf_ref, cv_ref,
                 *, col_chunk):
    i = pl.program_id(1)
    d = cur_ref.shape[1]
    halo = CONV_HALO
    buf_ref[0:halo, :] = jnp.where(i > 0, prev_ref[...], 0.0)
    buf_ref[halo:halo + CONV_ROWS, :] = cur_ref[...]
    first = halo - (CONV_WIDTH - 1)
    for c0 in range(0, d, col_chunk):
        cs = slice(c0, c0 + col_chunk)
        acc = jnp.zeros((CONV_ROWS, col_chunk), jnp.float32) + bdw_ref[:, cs]
        for s in range(SUBLANES):
            taps = [t for t in range(CONV_WIDTH) if (first + t) % SUBLANES == s]
            n_rows = CONV_ROWS + (SUBLANES if s else 0)
            part = None
            for t in taps:
                start = first + t - s
                wt = w_ref[t * SUBLANES:(t + 1) * SUBLANES, cs][None]
                term = buf_ref[start:start + n_rows, cs].reshape(-1, SUBLANES, col_chunk) * wt
                part = term if part is None else part + term
            part = part.reshape(n_rows, col_chunk)
            if s:
                part = pltpu.roll(part, n_rows - s, 0)[:CONV_ROWS, :]
            acc = acc + part
        cv_ref[:, cs] = acc
    x = cv_ref[...]
    mu = jnp.mean(x, axis=-1, keepdims=True)
    xc = x - mu
    var = jnp.mean(xc * xc, axis=-1, keepdims=True)
    y = xc * lax.rsqrt(var + LN_EPS) * g_ref[...] + b_ref[...]
    o_ref[...] = (y * jax.nn.sigmoid(y)).astype(o_ref.dtype)


def _conv_ln_swish(u, w_dw, b_dw, ln_g, ln_b, layer, batch, seq):
    t, d = u.shape
    nblk = seq // CONV_ROWS
    ratio = CONV_ROWS // CONV_HALO
    row3 = lambda v: v.reshape(v.shape[0], 1, d)
    vec = pl.BlockSpec((None, 1, d), lambda b, i: (layer, 0, 0))
    return pl.pallas_call(
        functools.partial(_conv_kernel, col_chunk=LANES),
        grid=(batch, nblk),
        in_specs=[pl.BlockSpec((CONV_HALO, d),
                               lambda b, i: (jnp.maximum((b * nblk + i) * ratio - 1, 0), 0)),
                  pl.BlockSpec((CONV_ROWS, d), lambda b, i: (b * nblk + i, 0)),
                  pl.BlockSpec((None, CONV_WIDTH * SUBLANES, d), lambda b, i: (layer, 0, 0)),
                  vec, vec, vec],
        out_specs=pl.BlockSpec((CONV_ROWS, d), lambda b, i: (b * nblk + i, 0)),
        out_shape=jax.ShapeDtypeStruct((t, d), jnp.bfloat16),
        scratch_shapes=[pltpu.VMEM((CONV_HALO + CONV_ROWS, d), jnp.float32),
                        pltpu.VMEM((CONV_ROWS, d), jnp.float32)],
        compiler_params=_params(("parallel", "parallel")),
        name="conv_ln_swish",
    )(u, u, jnp.repeat(w_dw, SUBLANES, axis=1), row3(b_dw), row3(ln_g), row3(ln_b))


def _pack_halves(x):
    half = x.shape[1] // 2
    bits = pltpu.bitcast(x.astype(jnp.bfloat16).astype(jnp.float32), jnp.uint32)
    return (bits[:, :half] >> 16) | bits[:, half:]


def _unpack_halves(words):
    lo = pltpu.bitcast(words << 16, jnp.float32)
    hi = pltpu.bitcast(words & jnp.uint32(0xFFFF0000), jnp.float32)
    return lo, hi


def _router_kernel(x_ref, g_ref, whi_ref, wlo_ref, idx_ref, wt_ref, xp_ref, *, n_experts):
    hn = _rms_rows(x_ref[...], g_ref[...])
    xp_ref[...] = _pack_halves(hn)
    hi = hn.astype(jnp.bfloat16)
    lo = (hn - hi.astype(jnp.float32)).astype(jnp.bfloat16)
    dot = functools.partial(jnp.dot, preferred_element_type=jnp.float32)
    logits = dot(hi, whi_ref[...]) + (dot(hi, wlo_ref[...]) + dot(lo, whi_ref[...]))
    lane = lax.broadcasted_iota(jnp.int32, logits.shape, 1)
    logits = jnp.where(lane < n_experts, logits, -jnp.inf)
    v1 = jnp.max(logits, axis=-1, keepdims=True)
    i1 = jnp.min(jnp.where(logits == v1, lane, LANES), axis=-1, keepdims=True)
    rest = jnp.where(lane == i1, -jnp.inf, logits)
    v2 = jnp.max(rest, axis=-1, keepdims=True)
    i2 = jnp.min(jnp.where(rest == v2, lane, LANES), axis=-1, keepdims=True)
    e2 = jnp.exp(v2 - v1)
    denom = 1.0 + e2
    idx_ref[...] = jnp.where(lane == 0, i1, i2)
    wt_ref[...] = jnp.where(lane == 0, 1.0 / denom, e2 / denom)


def _router(h, g, w_router, tm=256):
    t, d = h.shape
    n_experts = w_router.shape[1]
    tm = min(tm, t)
    w_pad = jnp.zeros((d, LANES), jnp.float32).at[:, :n_experts].set(w_router)
    w_hi = w_pad.astype(jnp.bfloat16)
    w_lo = (w_pad - w_hi.astype(jnp.float32)).astype(jnp.bfloat16)
    return pl.pallas_call(
        functools.partial(_router_kernel, n_experts=n_experts),
        grid=(t // tm,),
        in_specs=[pl.BlockSpec((tm, d), lambda i: (i, 0)),
                  pl.BlockSpec((1, d), lambda i: (0, 0)),
                  pl.BlockSpec((d, LANES), lambda i: (0, 0)),
                  pl.BlockSpec((d, LANES), lambda i: (0, 0))],
        out_specs=[pl.BlockSpec((tm, LANES), lambda i: (i, 0)),
                   pl.BlockSpec((tm, LANES), lambda i: (i, 0)),
                   pl.BlockSpec((tm, d // 2), lambda i: (i, 0))],
        out_shape=[jax.ShapeDtypeStruct((t, LANES), jnp.int32),
                   jax.ShapeDtypeStruct((t, LANES), jnp.float32),
                   jax.ShapeDtypeStruct((t, d // 2), jnp.uint32)],
        compiler_params=_params(("parallel",)),
        name="router",
    )(h, g.reshape(1, d), w_hi, w_lo)


def _row_copy(src_hbm, row, dst, slot, sem):
    return pltpu.make_async_copy(src_hbm.at[pl.ds(row, 1), :], dst.at[pl.ds(slot, 1), :], sem)


def _dispatch_kernel(src_ref, nu_ref, xp_hbm, o_ref, buf_ref, sem, *, rows, tiles_per_mm):
    step = pl.program_id(0)
    n_used = nu_ref[0] * tiles_per_mm
    half = buf_ref.shape[2]

    def gather(tile):
        slot = tile % 2
        base = tile * rows

        def issue(r, carry):
            _row_copy(xp_hbm, src_ref[base + r], buf_ref.at[slot], r, sem.at[slot]).start()
            return carry

        lax.fori_loop(0, rows, issue, 0, unroll=ISSUE_UNROLL)

    @pl.when(jnp.logical_and(step == 0, n_used > 0))
    def _():
        gather(step)

    @pl.when(step + 1 < n_used)
    def _():
        gather(step + 1)

    @pl.when(step < n_used)
    def _():
        slot = step % 2
        pltpu.make_async_copy(xp_hbm.at[pl.ds(0, rows), :], buf_ref.at[slot], sem.at[slot]).wait()
        lo, hi = _unpack_halves(buf_ref[slot])
        o_ref[:, :half] = lo.astype(o_ref.dtype)
        o_ref[:, half:] = hi.astype(o_ref.dtype)

    @pl.when(step >= n_used)
    def _():
        o_ref[...] = jnp.zeros(o_ref.shape, o_ref.dtype)


def _dispatch(xp, src, n_used, tm, rows=256):
    t, half = xp.shape
    p = src.shape[0]
    rows = min(rows, tm)
    return pl.pallas_call(
        functools.partial(_dispatch_kernel, rows=rows, tiles_per_mm=tm // rows),
        grid_spec=pltpu.PrefetchScalarGridSpec(
            num_scalar_prefetch=2,
            grid=(p // rows,),
            in_specs=[pl.BlockSpec(memory_space=pl.ANY)],
            out_specs=pl.BlockSpec((rows, 2 * half), lambda i, s, nu: (i, 0)),
            scratch_shapes=[pltpu.VMEM((2, rows, half), jnp.uint32),
                            pltpu.SemaphoreType.DMA((2,))]),
        out_shape=jax.ShapeDtypeStruct((p, 2 * half), jnp.bfloat16),
        compiler_params=_params(("arbitrary",)),
        name="moe_dispatch",
    )(src, n_used, xp)


def _combine_kernel(pos_ref, ys_hbm, h_ref, wt_ref, o_ref, buf_ref, sem, *, rows, tn):
    step = pl.program_id(0)
    n_steps = pl.num_programs(0)

    def gather(tile):
        slot = tile % 2
        base = tile * rows

        def issue(r, carry):
            for k in range(TOP_K):
                _row_copy(ys_hbm, pos_ref[TOP_K * (base + r) + k], buf_ref.at[slot, k], r,
                          sem.at[slot]).start()
            return carry

        lax.fori_loop(0, rows, issue, 0, unroll=ISSUE_UNROLL)

    @pl.when(step == 0)
    def _():
        gather(step)

    @pl.when(step + 1 < n_steps)
    def _():
        gather(step + 1)

    slot = step % 2
    for k in range(TOP_K):
        pltpu.make_async_copy(ys_hbm.at[pl.ds(0, rows), :], buf_ref.at[slot, k],
                              sem.at[slot]).wait()
    wt = wt_ref[...]
    half = tn // 2
    for jt in range(o_ref.shape[1] // tn):
        lo_cols = slice(jt * tn, jt * tn + half)
        hi_cols = slice(jt * tn + half, (jt + 1) * tn)
        acc_lo = h_ref[:, lo_cols]
        acc_hi = h_ref[:, hi_cols]
        for k in range(TOP_K):
            lo, hi = _unpack_halves(buf_ref[slot, k, :, jt * half:(jt + 1) * half])
            acc_lo = acc_lo + wt[:, k:k + 1] * lo
            acc_hi = acc_hi + wt[:, k:k + 1] * hi
        o_ref[:, lo_cols] = acc_lo
        o_ref[:, hi_cols] = acc_hi


def _combine(h, ys, pos, wts, tn, rows=128):
    t, d = h.shape
    rows = min(rows, t)
    return pl.pallas_call(
        functools.partial(_combine_kernel, rows=rows, tn=tn),
        grid_spec=pltpu.PrefetchScalarGridSpec(
            num_scalar_prefetch=1,
            grid=(t // rows,),
            in_specs=[pl.BlockSpec(memory_space=pl.ANY),
                      pl.BlockSpec((rows, d), lambda i, s: (i, 0)),
                      pl.BlockSpec((rows, LANES), lambda i, s: (i, 0))],
            out_specs=pl.BlockSpec((rows, d), lambda i, s: (i, 0)),
            scratch_shapes=[pltpu.VMEM((2, TOP_K, rows, d // 2), jnp.uint32),
                            pltpu.SemaphoreType.DMA((2,))]),
        out_shape=jax.ShapeDtypeStruct((t, d), jnp.float32),
        compiler_params=_params(("arbitrary",)),
        name="moe_combine",
    )(pos, ys, h, wts)


def _route_tables(idx, n_experts, tm):
    t = idx.shape[0]
    n_assign = t * TOP_K
    p = n_assign + n_experts * tm
    flat = idx.reshape(-1)
    onehot = (flat[:, None] == jnp.arange(n_experts)[None, :]).astype(jnp.int32)
    csum = jnp.cumsum(onehot, axis=0)
    counts = csum[-1]
    rank = jnp.sum(csum * onehot, axis=1) - 1
    padded = ((counts + tm - 1) // tm) * tm
    ends = jnp.cumsum(padded)
    starts = ends - padded
    pos = jnp.sum(starts[None, :] * onehot, axis=1) + rank
    src = jnp.zeros((p,), jnp.int32).at[pos].set(jnp.arange(n_assign, dtype=jnp.int32) // TOP_K)
    tile_start = jnp.arange(p // tm, dtype=jnp.int32) * tm
    tile_expert = jnp.sum((tile_start[:, None] >= ends[None, :]).astype(jnp.int32), axis=1)
    tile_expert = jnp.minimum(tile_expert, n_experts - 1)
    n_used = (ends[-1:] // tm).astype(jnp.int32)
    return pos.astype(jnp.int32), src, tile_expert.astype(jnp.int32), n_used


def _moe(h, g, w_router, w_gate, w_up, w_down, layer, tm=512):
    n_experts = w_router.shape[1]
    f = w_gate.shape[3]
    d = h.shape[1]
    idx_pad, wts, xp = _router(h, g, w_router)
    pos, src, tile_expert, n_used = _route_tables(idx_pad[:, :TOP_K], n_experts, tm)
    xs = _dispatch(xp, src, n_used, tm)
    tables = (tile_expert, n_used)
    act = _expert_matmul(xs, [w_gate, w_up], layer, tables, n_out=f, tm=tm, tn=512,
                         mode="silu_mul", out_dtype=jnp.bfloat16)
    tn_down = min(1024, d)
    ys = _expert_matmul(act, [w_down], layer, tables, n_out=d, tm=tm, tn=tn_down,
                        mode="packed")
    return _combine(h, ys, pos, wts, tn_down)


def kernel(x, norm_mix, norm_ffn, attn_w_qkv, attn_q_norm, attn_k_norm, attn_rel_bias,
           attn_w_out, conv_w_in, conv_b_in, conv_w_dw, conv_b_dw, conv_ln_g, conv_ln_b,
           conv_w_out, conv_b_out, ffn_w_gate_up, ffn_w_down, moe_w_router, moe_w_gate,
           moe_w_up, moe_w_down):
    batch, seq, d = x.shape
    depth = norm_mix.shape[0]
    d_ff = ffn_w_gate_up.shape[2] // 2
    bf16 = jnp.bfloat16
    row3 = lambda v: v.reshape(v.shape[0], 1, v.shape[1])
    conv_b_in3, conv_b_out3 = row3(conv_b_in), row3(conv_b_out)
    h = x.reshape(batch * seq, d)
    for i in range(depth):
        j = i // 2
        hn = _rmsnorm(h, norm_mix[i])
        if i % 2 == 0:
            gains = jnp.stack([attn_q_norm[j] * (HEAD_DIM ** -0.5 * LOG2E),
                               attn_k_norm[j]]).reshape(2, 1, HEAD_DIM)
            qkv = _dense_matmul(hn, [(attn_w_qkv, 0)], j, n_out=3 * d, tm=1024, tn=min(1024, d),
                                mode="headnorm", gains=gains, out_dtype=bf16)
            o = _attention(qkv, attn_rel_bias[j], batch, seq)
            h = _dense_matmul(o, [(attn_w_out, 0)], j, n_out=d, tm=1024, tn=512, res=h)
            hn = _rmsnorm(h, norm_ffn[i])
            act = _dense_matmul(hn, [(ffn_w_gate_up, 0), (ffn_w_gate_up, d_ff)], j, n_out=d_ff,
                                tm=1024, tn=512, mode="silu_mul", out_dtype=bf16)
            h = _dense_matmul(act, [(ffn_w_down, 0)], j, n_out=d, tm=512, tn=512, res=h)
        else:
            u = _dense_matmul(hn, [(conv_w_in, 0), (conv_w_in, d)], j, n_out=d, tm=1024,
                              tn=512, mode="mul_sig",
                              biases=[(conv_b_in3, 0), (conv_b_in3, d)])
            c = _conv_ln_swish(u, conv_w_dw, conv_b_dw, conv_ln_g, conv_ln_b, j, batch, seq)
            h = _dense_matmul(c, [(conv_w_out, 0)], j, n_out=d, tm=1024, tn=512,
                              biases=[(conv_b_out3, 0)], res=h)
            h = _moe(h, norm_ffn[i], moe_w_router[j], moe_w_gate, moe_w_up, moe_w_down, j)
    return h.reshape(batch, seq, d)
```
